```python
import jax, jax.numpy as jnp
from jax import lax
import numpy as np

D_MODEL = 2048
BATCH = 4
SEQ = 2048
DEPTH = 4
DEC_BATCH = 128
DEC_SEQ = 1
PAST_LEN = 16384
PAGE_SIZE = 128

N_MEM = 256
X_HEADS = 4
X_HEAD_DIM = 128
X_DIM = X_HEADS * X_HEAD_DIM
MIX = D_MODEL - X_DIM
A_HEAD_DIM = 128
A_HEADS = MIX // A_HEAD_DIM
A_CHUNK = 64
A_MIX_COLS = 4 * MIX
A_IN = A_MIX_COLS + X_DIM
F_FLOOR = 1e-30
B_HEAD_DIM = 64
B_HEADS = MIX // B_HEAD_DIM
B_DECAY_LORA = 64
B_AAA_LORA = 64
B_GATE_LORA = 224
B_MIX_COLS = 3 * MIX + B_DECAY_LORA + B_AAA_LORA + B_GATE_LORA
B_IN = B_MIX_COLS + X_DIM
D_FF = 4 * D_MODEL
N_A_LAYERS = (DEPTH + 1) // 2
N_B_LAYERS = DEPTH // 2
RMS_EPS = 1e-6
GN_EPS = 64e-5

kernel_name = 'hybrid_hgrn2_rwkv7_memxattn_step'


def rms_norm(x, g):
    xf = x.astype(jnp.float32)
    y = xf * lax.rsqrt(jnp.mean(xf * xf, axis=-1, keepdims=True) + RMS_EPS)
    return (y * g).astype(x.dtype)


def hgrn_lower_bounds(logits):
    p = jax.nn.softmax(logits.astype(jnp.float32), axis=0)
    return jnp.cumsum(p, axis=0) - p[0]


def cross_attend(q, mem_k, mem_v):
    B, T, _ = q.shape
    qh = q.reshape(B, T, X_HEADS, X_HEAD_DIM)
    s = jnp.einsum('bthe,bmhe->bhtm', qh, mem_k).astype(jnp.float32) * (X_HEAD_DIM ** -0.5)
    p = jax.nn.softmax(s, axis=-1).astype(mem_v.dtype)
    return jnp.einsum('bhtm,bmhe->bthe', p, mem_v).reshape(B, T, X_DIM)


def hgrn2_chunk_scan(q, k, v, log_f, s0):
    B, T, H, K = q.shape
    V = v.shape[-1]
    C = min(A_CHUNK, T)
    pad = (-T) % C
    f32 = jnp.float32
    q, k, v, log_f = (t.astype(f32) for t in (q, k, v, log_f))
    if pad:
        widths = ((0, 0), (0, pad), (0, 0), (0, 0))
        q, k, v, log_f = (jnp.pad(t, widths) for t in (q, k, v, log_f))
    n = (T + pad) // C

    def to_chunks(t):
        return t.reshape(B, n, C, H, t.shape[-1]).transpose(1, 0, 3, 2, 4)

    causal = jnp.tril(jnp.ones((C, C), dtype=bool))[:, :, None]

    def step(s, inp):
        qc, kc, vc, gc = inp
        b = jnp.cumsum(gc, axis=2)
        diff = b[:, :, :, None, :] - b[:, :, None, :, :]
        dec = jnp.where(causal, jnp.exp(jnp.where(causal, diff, 0.0)), 0.0)
        att = jnp.einsum('bhtk,bhsk,bhtsk->bhts', qc, kc, dec)
        o = jnp.einsum('bhts,bhsv->bhtv', att, vc) + jnp.einsum('bhtk,bhkv->bhtv', qc * jnp.exp(b), s)
        b_end = b[:, :, -1:, :]
        s = s * jnp.exp(b_end[:, :, 0, :, None]) + jnp.einsum('bhsk,bhsv->bhkv', kc * jnp.exp(b_end - b), vc)
        return s, o

    s_T, o = lax.scan(step, s0.astype(f32), tuple(to_chunks(t) for t in (q, k, v, log_f)))
    o = o.transpose(1, 0, 3, 2, 4).reshape(B, n * C, H, V)[:, :T]
    return o, s_T


def hgrn2_mix(P, lb, onorm_g, s0):
    B, T, _ = P.shape
    q, f, i, g = jnp.split(P, 4, axis=-1)
    ff = f.astype(jnp.float32)
    f_gate = lb + (1.0 - lb) * jax.nn.sigmoid(ff)
    log_f = jnp.log(jnp.maximum(f_gate, F_FLOOR))
    k = (1.0 - lb) * jax.nn.sigmoid(-ff)
    heads = lambda t: t.reshape(B, T, A_HEADS, A_HEAD_DIM)
    o, s_T = hgrn2_chunk_scan(heads(jax.nn.silu(q)), heads(k), heads(i), heads(log_f), s0)
    o = rms_norm(o, onorm_g.reshape(A_HEADS, A_HEAD_DIM)).reshape(B, T, MIX).astype(P.dtype)
    return o * jax.nn.silu(g), s_T.astype(s0.dtype)


def rwkv7_mix(P, P_prev, mu, w0, w2, a0, a2, g2, k_k, k_a, r_k, lnx_g, lnx_b, s0):
    B, T, _ = P.shape
    f32 = jnp.float32
    xs = P + (P_prev - P) * mu
    r, k, v, wl, al, gl = jnp.split(
        xs, [MIX, 2 * MIX, 3 * MIX, 3 * MIX + B_DECAY_LORA, 3 * MIX + B_DECAY_LORA + B_AAA_LORA], axis=-1)
    w_raw = -jax.nn.softplus(-(w0 + jnp.tanh(wl) @ w2).astype(f32)) - 0.5
    decay = jnp.exp(-jnp.exp(w_raw))
    a = jax.nn.sigmoid((a0 + al @ a2).astype(f32))
    g = jax.nn.sigmoid(gl) @ g2
    heads = lambda t: t.reshape(B, T, B_HEADS, B_HEAD_DIM)
    ph = lambda t: t.reshape(B_HEADS, B_HEAD_DIM)
    rf, kf, vf, a, decay = (heads(t.astype(f32)) for t in (r, k, v, a, decay))
    kk = kf * ph(k_k)
    kk = kk / jnp.maximum(jnp.sqrt(jnp.sum(kk * kk, axis=-1, keepdims=True)), 1e-12)
    kf = kf * (1.0 + (a - 1.0) * ph(k_a))

    def step(S, inp):
        r_t, k_t, v_t, kk_t, a_t, w_t = inp
        sa = jnp.einsum('bhvk,bhk->bhv', S, -kk_t)
        S = (S * w_t[:, :, None, :] + sa[..., None] * (kk_t * a_t)[:, :, None, :]
             + v_t[..., None] * k_t[:, :, None, :])
        return S, jnp.einsum('bhvk,bhk->bhv', S, r_t)

    seqs = tuple(t.transpose(1, 0, 2, 3) for t in (rf, kf, vf, kk, a, decay))
    S_T, o = lax.scan(step, s0.astype(f32), seqs)
    o = o.transpose(1, 0, 2, 3)
    mean = jnp.mean(o, axis=-1, keepdims=True)
    var = jnp.mean(jnp.square(o - mean), axis=-1, keepdims=True)
    on = (o - mean) * lax.rsqrt(var + GN_EPS) * ph(lnx_g) + ph(lnx_b)
    bonus = jnp.sum(rf * kf * r_k, axis=-1, keepdims=True) * vf
    out = (on + bonus).reshape(B, T, MIX).astype(P.dtype) * g
    return out, S_T.astype(s0.dtype)


def setup_inputs(seed: int = 0) -> dict:
    key = jax.random.key(seed)
    ks = iter(jax.random.split(key, 48))
    f32 = jnp.float32
    nrm = lambda shape, scale: jax.random.normal(next(ks), shape, f32) * scale
    gain = lambda shape: 1.0 + 0.02 * jax.random.normal(next(ks), shape, f32)
    return {
        'x_prompt': nrm((BATCH, SEQ, D_MODEL), 1.0),
        'x_sample': nrm((DEC_BATCH, DEC_SEQ, D_MODEL), 1.0),
        'cache_mem_k': nrm((DEPTH, DEC_BATCH, N_MEM, X_HEADS, X_HEAD_DIM), 1.0),
        'cache_mem_v': nrm((DEPTH, DEC_BATCH, N_MEM, X_HEADS, X_HEAD_DIM), 1.0),
        'state_hgrn': nrm((N_A_LAYERS, DEC_BATCH, A_HEADS, A_HEAD_DIM, A_HEAD_DIM), 0.5),
        'state_rwkv': nrm((N_B_LAYERS, DEC_BATCH, B_HEADS, B_HEAD_DIM, B_HEAD_DIM), 0.5),
        'state_rwkv_shift': nrm((N_B_LAYERS, DEC_BATCH, D_MODEL), 1.0),
        'mem_prompt': nrm((BATCH, N_MEM, D_MODEL), 1.0),
        'attn_norm_g': gain((DEPTH, D_MODEL)),
        'mlp_norm_g': gain((DEPTH, D_MODEL)),
        'final_norm_g': gain((D_MODEL,)),
        'mem_norm_g': gain((D_MODEL,)),
        'wk_mem': nrm((DEPTH, D_MODEL, X_DIM), D_MODEL ** -0.5),
        'wv_mem': nrm((DEPTH, D_MODEL, X_DIM), D_MODEL ** -0.5),
        'a_w_in': nrm((N_A_LAYERS, D_MODEL, A_IN), D_MODEL ** -0.5),
        'a_w_out': nrm((N_A_LAYERS, D_MODEL, D_MODEL), D_MODEL ** -0.5),
        'a_lb_logits': nrm((N_A_LAYERS, MIX), 1.0),
        'a_onorm_g': gain((N_A_LAYERS, MIX)),
        'b_w_in': nrm((N_B_LAYERS, D_MODEL, B_IN), D_MODEL ** -0.5),
        'b_w_out': nrm((N_B_LAYERS, D_MODEL, D_MODEL), D_MODEL ** -0.5),
        'b_mu': jax.random.uniform(next(ks), (N_B_LAYERS, B_MIX_COLS), f32),
        'b_w0': nrm((N_B_LAYERS, MIX), 0.5),
        'b_w2': nrm((N_B_LAYERS, B_DECAY_LORA, MIX), 0.5 * B_DECAY_LORA ** -0.5),
        'b_a0': nrm((N_B_LAYERS, MIX), 0.1),
        'b_a2': nrm((N_B_LAYERS, B_AAA_LORA, MIX), 0.5 * B_AAA_LORA ** -0.5),
        'b_g2': nrm((N_B_LAYERS, B_GATE_LORA, MIX), B_GATE_LORA ** -0.5),
        'b_k_k': 0.85 * gain((N_B_LAYERS, MIX)),
        'b_k_a': gain((N_B_LAYERS, MIX)),
        'b_r_k': nrm((N_B_LAYERS, B_HEADS, B_HEAD_DIM), 0.1),
        'b_lnx_g': gain((N_B_LAYERS, MIX)),
        'b_lnx_b': nrm((N_B_LAYERS, MIX), 0.01),
        'mlp_w1': nrm((DEPTH, D_MODEL, D_FF), D_MODEL ** -0.5),
        'mlp_w2': nrm((DEPTH, D_FF, D_MODEL), 0.5 * D_FF ** -0.5),
    }


def reference(x_prompt, x_sample, cache_mem_k, cache_mem_v, state_hgrn, state_rwkv, state_rwkv_shift,
              mem_prompt, attn_norm_g, mlp_norm_g, final_norm_g, mem_norm_g, wk_mem, wv_mem,
              a_w_in, a_w_out, a_lb_logits, a_onorm_g,
              b_w_in, b_w_out, b_mu, b_w0, b_w2, b_a0, b_a2, b_g2, b_k_k, b_k_a, b_r_k, b_lnx_g, b_lnx_b,
              mlp_w1, mlp_w2):
    lower_bounds = hgrn_lower_bounds(a_lb_logits)

    def trunk(x, mem_k, mem_v, s_a, s_b, shift):
        new_a, new_b, new_shift = [], [], []
        for layer in range(DEPTH):
            j = layer // 2
            h = rms_norm(x, attn_norm_g[layer])
            if layer % 2 == 0:
                proj = h @ a_w_in[j]
                mix, s_new = hgrn2_mix(proj[..., :A_MIX_COLS], lower_bounds[j], a_onorm_g[j], s_a[j])
                new_a.append(s_new)
                w_out = a_w_out[j]
            else:
                h_ext = jnp.concatenate([shift[j][:, None, :].astype(h.dtype), h], axis=1)
                proj_ext = h_ext @ b_w_in[j]
                proj = proj_ext[:, 1:]
                mix, s_new = rwkv7_mix(proj[..., :B_MIX_COLS], proj_ext[:, :-1, :B_MIX_COLS], b_mu[j],
                                       b_w0[j], b_w2[j], b_a0[j], b_a2[j], b_g2[j], b_k_k[j], b_k_a[j],
                                       b_r_k[j], b_lnx_g[j], b_lnx_b[j], s_b[j])
                new_b.append(s_new)
                new_shift.append(h[:, -1])
                w_out = b_w_out[j]
            xo = cross_attend(proj[..., -X_DIM:], mem_k[layer], mem_v[layer])
            x = x + jnp.concatenate([mix, xo.astype(mix.dtype)], axis=-1) @ w_out
            u = rms_norm(x, mlp_norm_g[layer]) @ mlp_w1[layer]
            x = x + jnp.square(jax.nn.relu(u)) @ mlp_w2[layer]
        return rms_norm(x, final_norm_g), jnp.stack(new_a), jnp.stack(new_b), jnp.stack(new_shift)

    bp, n_mem = mem_prompt.shape[0], mem_prompt.shape[1]
    m = rms_norm(mem_prompt, mem_norm_g)
    mem_k_prompt = jnp.einsum('bmd,lde->lbme', m, wk_mem).reshape(DEPTH, bp, n_mem, X_HEADS, X_HEAD_DIM)
    mem_v_prompt = jnp.einsum('bmd,lde->lbme', m, wv_mem).reshape(DEPTH, bp, n_mem, X_HEADS, X_HEAD_DIM)
    zeros_a = jnp.zeros((N_A_LAYERS, bp, A_HEADS, A_HEAD_DIM, A_HEAD_DIM), x_prompt.dtype)
    zeros_b = jnp.zeros((N_B_LAYERS, bp, B_HEADS, B_HEAD_DIM, B_HEAD_DIM), x_prompt.dtype)
    zeros_shift = jnp.zeros((N_B_LAYERS, bp, D_MODEL), x_prompt.dtype)
    y_prompt, state_hgrn_prompt, state_rwkv_prompt, state_rwkv_shift_prompt = trunk(
        x_prompt, mem_k_prompt, mem_v_prompt, zeros_a, zeros_b, zeros_shift)

    y_sample, state_hgrn_sample, state_rwkv_sample, state_rwkv_shift_sample = trunk(
        x_sample, cache_mem_k, cache_mem_v, state_hgrn, state_rwkv, state_rwkv_shift)

    return (y_prompt, y_sample, mem_k_prompt, mem_v_prompt, state_hgrn_prompt, state_rwkv_prompt,
            state_rwkv_shift_prompt, state_hgrn_sample, state_rwkv_sample, state_rwkv_shift_sample)
```

```python
import functools

import jax
import jax.numpy as jnp
from jax import lax
from jax.experimental import pallas as pl
from jax.experimental.pallas import tpu as pltpu

F32 = jnp.float32
BF16 = jnp.bfloat16

RMS_EPS = 1e-6
GN_EPS = 64e-5
F_FLOOR = 1e-30
X_HEADS = 4
X_HEAD_DIM = 128
A_HEAD_DIM = 128
B_HEAD_DIM = 64
B_DECAY_LORA = 64
B_AAA_LORA = 64
B_GATE_LORA = 224

LANES = 128
VMEM_LIMIT = 56 * 1024 * 1024


def _cparams(*sem):
    return pltpu.CompilerParams(dimension_semantics=sem, vmem_limit_bytes=VMEM_LIMIT)


def _tile(n, cap, mult=8):
    if n <= cap:
        return n
    best = None
    for d in range(mult, cap + 1, mult):
        if n % d == 0:
            best = d
    assert best is not None, (n, cap, mult)
    return best


def _bdot(a, b):
    return jnp.dot(a.astype(BF16), b.astype(BF16), preferred_element_type=F32)


def _bdot_nt(a, b):
    return lax.dot_general(a.astype(BF16), b.astype(BF16), (((1,), (1,)), ((), ())),
                           preferred_element_type=F32)


def _bdot_tn(a, b):
    return lax.dot_general(a.astype(BF16), b.astype(BF16), (((0,), (0,)), ((), ())),
                           preferred_element_type=F32)


def _split3(x):
    hi = x.astype(BF16)
    r1 = x - hi.astype(F32)
    mid = r1.astype(BF16)
    lo = (r1 - mid.astype(F32)).astype(BF16)
    return hi, mid, lo


def _exact_ldot(m01, x):
    hi, mid, lo = _split3(x)
    d = lambda p: jnp.dot(m01, p, preferred_element_type=F32)
    return d(hi) + d(mid) + d(lo)


def _sigmoid(x):
    return 1.0 / (1.0 + jnp.exp(-x))


def _silu(x):
    return x * _sigmoid(x)


def _iota2(shape, axis):
    return lax.broadcasted_iota(jnp.int32, shape, axis)


def _rmsnorm_kernel(x_ref, g_ref, o_ref):
    x = x_ref[...]
    y = x * lax.rsqrt(jnp.mean(x * x, axis=-1, keepdims=True) + RMS_EPS)
    o_ref[...] = (y * g_ref[...]).astype(o_ref.dtype)


def _rmsnorm(x, g, out_dtype):
    m, d = x.shape
    tm = _tile(m, 512, 16)
    return pl.pallas_call(
        _rmsnorm_kernel,
        out_shape=jax.ShapeDtypeStruct((m, d), out_dtype),
        grid=(m // tm,),
        in_specs=[pl.BlockSpec((tm, d), lambda i: (i, 0)),
                  pl.BlockSpec((1, d), lambda i: (0, 0))],
        out_specs=pl.BlockSpec((tm, d), lambda i: (i, 0)),
        compiler_params=_cparams("parallel"),
        name="rmsnorm",
    )(x, g.reshape(1, d))


def _matmul_kernel(*refs, nk1, nk, act, has_res, has_a2):
    it = iter(refs)
    a1_ref = next(it)
    a2_ref = next(it) if has_a2 else None
    w_ref = next(it)
    res_ref = next(it) if has_res else None
    o_ref = next(it)
    acc_ref = next(it) if nk > 1 else None

    def finish(acc):
        if act == "relu2":
            acc = jnp.square(jnp.maximum(acc, 0.0))
        if has_res:
            acc = res_ref[...] + acc
        o_ref[...] = acc.astype(o_ref.dtype)

    if nk == 1:
        finish(_bdot(a1_ref[...], w_ref[...]))
        return

    k = pl.program_id(2)

    @pl.when(k == 0)
    def _():
        acc_ref[...] = jnp.zeros_like(acc_ref)

    if has_a2:
        @pl.when(k < nk1)
        def _():
            acc_ref[...] += _bdot(a1_ref[...], w_ref[...])

        @pl.when(k >= nk1)
        def _():
            acc_ref[...] += _bdot(a2_ref[...], w_ref[...])
    else:
        acc_ref[...] += _bdot(a1_ref[...], w_ref[...])

    @pl.when(k == nk - 1)
    def _():
        finish(acc_ref[...])


def _matmul(a1, w, *, layer=None, a2=None, res=None, act=None, out_dtype=F32, n_out=None,
            tm_cap=1056, tn_cap=512, tk_cap=2048, name="matmul"):
    m, k1 = a1.shape
    k2 = a2.shape[1] if a2 is not None else 0
    kdim = k1 + k2
    if layer is None:
        assert w.shape[0] == kdim
        w_block = lambda tk, tn: pl.BlockSpec((tk, tn), lambda i, j, k: (k, j))
    else:
        assert w.shape[1] == kdim
        w_block = lambda tk, tn: pl.BlockSpec((None, tk, tn), lambda i, j, k: (layer, k, j))
    n = w.shape[-1] if n_out is None else n_out
    tm = _tile(m, tm_cap, 16 if a1.dtype == BF16 or out_dtype == BF16 else 8)
    tn = _tile(n, tn_cap, LANES)
    if a2 is None:
        tk = _tile(kdim, tk_cap, LANES)
    else:
        tk = k2
        assert k1 % tk == 0 and tk % LANES == 0
    nk1, nk = k1 // tk, kdim // tk
    in_specs = [pl.BlockSpec((tm, tk), lambda i, j, k: (i, jnp.minimum(k, nk1 - 1)))]
    args = [a1]
    if a2 is not None:
        in_specs.append(pl.BlockSpec((tm, tk), lambda i, j, k: (i, 0)))
        args.append(a2)
    in_specs.append(w_block(tk, tn))
    args.append(w)
    if res is not None:
        in_specs.append(pl.BlockSpec((tm, tn), lambda i, j, k: (i, j)))
        args.append(res)
    kern = functools.partial(_matmul_kernel, nk1=nk1, nk=nk, act=act,
                             has_res=res is not None, has_a2=a2 is not None)
    return pl.pallas_call(
        kern,
        out_shape=jax.ShapeDtypeStruct((m, n), out_dtype),
        grid=(m // tm, n // tn, nk),
        in_specs=in_specs,
        out_specs=pl.BlockSpec((tm, tn), lambda i, j, k: (i, j)),
        scratch_shapes=[pltpu.VMEM((tm, tn), F32)] if nk > 1 else [],
        compiler_params=_cparams("parallel", "parallel", "arbitrary"),
        name=name,
    )(*args)


def _lower_bounds_kernel(lg_ref, o_ref):
    lg = lg_ref[...]
    e = jnp.exp(lg - jnp.max(lg, axis=0, keepdims=True))
    p = e / jnp.sum(e, axis=0, keepdims=True)
    run = jnp.zeros_like(p[0:1])
    for l in range(lg.shape[0]):
        run = run + p[l:l + 1]
        o_ref[l:l + 1, :] = run - p[0:1]


def _lower_bounds(logits):
    return pl.pallas_call(
        _lower_bounds_kernel,
        out_shape=jax.ShapeDtypeStruct(logits.shape, F32),
        name="hgrn_lower_bounds",
    )(logits)


def _softmax_rows(s):
    e = jnp.exp(s - jnp.max(s, axis=-1, keepdims=True))
    return e / jnp.sum(e, axis=-1, keepdims=True)


def _xattn_prompt_kernel(q_ref, k_ref, v_ref, o_ref):
    scale = X_HEAD_DIM ** -0.5
    for h in range(X_HEADS):
        sl = slice(h * X_HEAD_DIM, (h + 1) * X_HEAD_DIM)
        p = _softmax_rows(_bdot_nt(q_ref[:, sl], k_ref[:, sl]) * scale)
        o_ref[:, sl] = _bdot(p, v_ref[:, sl]).astype(o_ref.dtype)


def _xattn_prompt(q_arr, q_blk, mem_k, mem_v, nb, t):
    xdim = mem_k.shape[-1]
    n_mem = mem_k.shape[1]
    tq = _tile(t, 512, 16)
    nt = t // tq
    return pl.pallas_call(
        _xattn_prompt_kernel,
        out_shape=jax.ShapeDtypeStruct((nb * t, xdim), BF16),
        grid=(nb, nt),
        in_specs=[pl.BlockSpec((tq, xdim), lambda b, i: (b * nt + i, q_blk)),
                  pl.BlockSpec((None, n_mem, xdim), lambda b, i: (b, 0, 0)),
                  pl.BlockSpec((None, n_mem, xdim), lambda b, i: (b, 0, 0))],
        out_specs=pl.BlockSpec((tq, xdim), lambda b, i: (b * nt + i, 0)),
        compiler_params=_cparams("parallel", "parallel"),
        name="xattn_prompt",
    )(q_arr, mem_k, mem_v)


def _xattn_sample_kernel(q_ref, k_ref, v_ref, o_ref, *, bb):
    scale = X_HEAD_DIM ** -0.5
    for b in range(bb):
        for h in range(X_HEADS):
            sl = slice(h * X_HEAD_DIM, (h + 1) * X_HEAD_DIM)
            q = jnp.broadcast_to(q_ref[b:b + 1, sl], (8, X_HEAD_DIM))
            p = _softmax_rows(_bdot_nt(q, k_ref[b, :, sl]) * scale)
            o_ref[b:b + 1, sl] = _bdot(p, v_ref[b, :, sl])[0:1]


def _xattn_sample(q_arr, q_blk, cache_k, cache_v, layer):
    _, nb, n_mem, xdim = cache_k.shape
    bb = 8
    kv_spec = pl.BlockSpec((None, bb, n_mem, xdim), lambda i: (layer, i, 0, 0))
    return pl.pallas_call(
        functools.partial(_xattn_sample_kernel, bb=bb),
        out_shape=jax.ShapeDtypeStruct((nb, xdim), F32),
        grid=(nb // bb,),
        in_specs=[pl.BlockSpec((bb, xdim), lambda i: (i, q_blk)), kv_spec, kv_spec],
        out_specs=pl.BlockSpec((bb, xdim), lambda i: (i, 0)),
        compiler_params=_cparams("parallel"),
        name="xattn_sample",
    )(q_arr, cache_k, cache_v)


HG_TB = 256
HG_C = 16


def _hgrn_prompt_kernel(q_ref, f_ref, i_ref, g_ref, lb_ref, og_ref, mix_ref, st_ref,
                        s_ref, qs_ref, kk_ref, bl_ref, o_ref, *, tb, c):
    t = pl.program_id(2)

    @pl.when(t == 0)
    def _():
        s_ref[...] = jnp.zeros_like(s_ref)

    lb = lb_ref[...]
    ff = f_ref[...]
    lg = jnp.log(jnp.maximum(lb + (1.0 - lb) * _sigmoid(ff), F_FLOOR))
    kk_ref[...] = (1.0 - lb) * _sigmoid(-ff)
    qs_ref[...] = _silu(q_ref[...])
    shift = c.bit_length() - 1
    ri = _iota2((tb, tb), 0)
    ci = _iota2((tb, tb), 1)
    same_chunk = jnp.right_shift(ri, shift) == jnp.right_shift(ci, shift)
    lmat = jnp.where(same_chunk & (ci <= ri), 1.0, 0.0).astype(BF16)
    bl_ref[...] = _exact_ldot(lmat, lg)
    eye = (_iota2((LANES, LANES), 0) == _iota2((LANES, LANES), 1)).astype(F32)
    rowi = _iota2((c, 1), 0)

    def chunk(idx, carry):
        r0 = pl.multiple_of(idx * c, c)
        rows = pl.ds(r0, c)
        blc = bl_ref[rows, :]
        qc = qs_ref[rows, :]
        kc = kk_ref[rows, :]
        vc = i_ref[rows, :]
        bend = bl_ref[pl.ds(r0 + c - 1, 1), :]
        s = s_ref[...]
        o = _bdot(qc * jnp.exp(blc), s)
        for j in range(c):
            bs = bl_ref[pl.ds(r0 + j, 1), :]
            ks = kk_ref[pl.ds(r0 + j, 1), :]
            vs = i_ref[pl.ds(r0 + j, 1), :]
            d = jnp.exp(jnp.minimum(blc - bs, 0.0))
            a = jnp.sum(qc * ks * d, axis=-1, keepdims=True)
            o = o + jnp.where(rowi >= j, a, 0.0) * vs
        o_ref[rows, :] = o
        ecol = jnp.sum(eye * jnp.exp(bend), axis=-1, keepdims=True)
        s_ref[...] = s * ecol + _bdot_tn(kc * jnp.exp(bend - blc), vc)
        return carry

    lax.fori_loop(0, tb // c, chunk, 0)
    o = o_ref[...]
    y = o * lax.rsqrt(jnp.mean(o * o, axis=-1, keepdims=True) + RMS_EPS) * og_ref[...]
    mix_ref[...] = (y * _silu(g_ref[...])).astype(mix_ref.dtype)

    @pl.when(t == pl.num_programs(2) - 1)
    def _():
        st_ref[...] = s_ref[...]


def _hgrn_prompt(p, lb, og, nb, t):
    mix = lb.shape[0]
    hd = A_HEAD_DIM
    nh = mix // hd
    tb = _tile(t, HG_TB, HG_C)
    nt = t // tb
    col = lambda off: pl.BlockSpec((tb, hd), lambda b, h, i: (b * nt + i, off + h))
    vec = pl.BlockSpec((1, hd), lambda b, h, i: (0, h))
    return pl.pallas_call(
        functools.partial(_hgrn_prompt_kernel, tb=tb, c=HG_C),
        out_shape=(jax.ShapeDtypeStruct((nb * t, mix), BF16),
                   jax.ShapeDtypeStruct((nb, nh, hd, hd), F32)),
        grid=(nb, nh, nt),
        in_specs=[col(0), col(nh), col(2 * nh), col(3 * nh), vec, vec],
        out_specs=(pl.BlockSpec((tb, hd), lambda b, h, i: (b * nt + i, h)),
                   pl.BlockSpec((None, None, hd, hd), lambda b, h, i: (b, h, 0, 0))),
        scratch_shapes=[pltpu.VMEM((hd, hd), F32)] + [pltpu.VMEM((tb, hd), F32)] * 4,
        compiler_params=_cparams("parallel", "parallel", "arbitrary"),
        name="hgrn_prompt",
    )(p, p, p, p, lb.reshape(1, mix), og.reshape(1, mix))


def _hgrn_sample_kernel(p_ref, lb_ref, og_ref, s_ref, mix_ref, so_ref, *, bb):
    hd = s_ref.shape[-1]
    eye = (_iota2((hd, hd), 0) == _iota2((hd, hd), 1)).astype(F32)
    lb = lb_ref[...]
    col = lambda r: jnp.sum(eye[None] * r, axis=-1, keepdims=True)

    def body(b, carry):
        q, f, iv, g = p_ref[b, 0], p_ref[b, 1], p_ref[b, 2], p_ref[b, 3]
        dec = jnp.maximum(lb + (1.0 - lb) * _sigmoid(f), F_FLOOR)
        k = (1.0 - lb) * _sigmoid(-f)
        sn = s_ref[b] * col(dec) + col(k) * iv
        so_ref[b] = sn
        o = jnp.sum(sn * col(_silu(q)), axis=1, keepdims=True)
        y = o * lax.rsqrt(jnp.mean(o * o, axis=-1, keepdims=True) + RMS_EPS) * og_ref[...]
        mix_ref[b] = y * _silu(g)
        return carry

    lax.fori_loop(0, bb, body, 0)


def _hgrn_sample(p, lb, og, states, layer):
    mix = lb.shape[0]
    hd = A_HEAD_DIM
    nh = mix // hd
    nb = p.shape[0]
    bb = 4
    p5 = p[:, :4 * mix].reshape(nb, 4, nh, 1, hd)
    vec = pl.BlockSpec((nh, 1, hd), lambda i: (0, 0, 0))
    out, new_states = pl.pallas_call(
        functools.partial(_hgrn_sample_kernel, bb=bb),
        out_shape=(jax.ShapeDtypeStruct((nb, nh, 1, hd), F32),
                   jax.ShapeDtypeStruct((nb, nh, hd, hd), F32)),
        grid=(nb // bb,),
        in_specs=[pl.BlockSpec((bb, 4, nh, 1, hd), lambda i: (i, 0, 0, 0, 0)), vec, vec,
                  pl.BlockSpec((None, bb, nh, hd, hd), lambda i: (layer, i, 0, 0, 0))],
        out_specs=(pl.BlockSpec((bb, nh, 1, hd), lambda i: (i, 0, 0, 0)),
                   pl.BlockSpec((bb, nh, hd, hd), lambda i: (i, 0, 0, 0))),
        compiler_params=_cparams("parallel"),
        name="hgrn_sample",
    )(p5, lb.reshape(nh, 1, hd), og.reshape(nh, 1, hd), states)
    return out.reshape(nb, mix), new_states


RW_TR = 128
RW_TB = 256
RW_C = 64


def _softplus(x):
    return jnp.maximum(x, 0.0) + jnp.log(1.0 + jnp.exp(-jnp.abs(x)))


def _seg_sum(x, seg):
    lo = _iota2(x.shape, x.ndim - 1) < seg
    s0 = jnp.sum(jnp.where(lo, x, 0.0), axis=-1, keepdims=True)
    s1 = jnp.sum(jnp.where(lo, 0.0, x), axis=-1, keepdims=True)
    return jnp.where(lo, s0, s1)


def _rwkv_prep_kernel(*refs, mix, seq):
    if seq:
        (p_ref, mu_ref, w0_ref, a0_ref, kkw_ref, ka_ref, w2_ref, a2_ref, g2_ref,
         r_o, k_o, v_o, kk_o, kka_o, w_o, g_o, carry_ref) = refs
    else:
        (p_ref, prev_ref, mu_ref, w0_ref, a0_ref, kkw_ref, ka_ref, w2_ref, a2_ref, g2_ref,
         r_o, k_o, v_o, kk_o, kka_o, w_o, g_o) = refs
    p = p_ref[...]
    tr = p.shape[0]
    if seq:
        @pl.when(pl.program_id(1) == 0)
        def _():
            carry_ref[...] = jnp.zeros_like(carry_ref)

        prev = jnp.where(_iota2(p.shape, 0) == 0, carry_ref[...], pltpu.roll(p, 1, axis=0))
        carry_ref[...] = p[tr - 1:tr, :]
    else:
        prev = prev_ref[...]
    xs = p + (prev - p) * mu_ref[...]
    r = xs[:, :mix]
    k = xs[:, mix:2 * mix]
    v = xs[:, 2 * mix:3 * mix]
    la = xs[:, 3 * mix:3 * mix + LANES]
    gl = xs[:, 3 * mix + LANES:3 * mix + 3 * LANES]
    w_raw = -_softplus(-(w0_ref[...] + _bdot(jnp.tanh(la), w2_ref[...]))) - 0.5
    a = _sigmoid(a0_ref[...] + _bdot(la, a2_ref[...]))
    r_o[...] = r
    v_o[...] = v
    w_o[...] = jnp.exp(-jnp.exp(w_raw))
    g_o[...] = _bdot(_sigmoid(gl), g2_ref[...])
    k_o[...] = k * (1.0 + (a - 1.0) * ka_ref[...])
    for j in range(mix // LANES):
        sl = slice(j * LANES, (j + 1) * LANES)
        kk = k[:, sl] * kkw_ref[:, sl]
        kk = kk / jnp.maximum(jnp.sqrt(_seg_sum(kk * kk, B_HEAD_DIM)), 1e-12)
        kk_o[:, sl] = kk
        kka_o[:, sl] = kk * a[:, sl]


def _rwkv_prep(p, prev, params, nb, t):
    mu, w0, a0, kkw, ka, w2p, a2p, g2p = params
    mix = w0.shape[-1]
    rows, ncol = p.shape
    seq = prev is None
    tr = _tile(t, RW_TR, 8) if seq else _tile(rows, RW_TR, 8)
    nt = t // tr if seq else rows // tr
    if seq:
        grid = (nb, nt)
        rmap = lambda b, i: (b * nt + i, 0)
        cmap = lambda b, i: (0, 0)
        sem = ("parallel", "arbitrary")
    else:
        grid = (nt,)
        rmap = lambda i: (i, 0)
        cmap = lambda i: (0, 0)
        sem = ("parallel",)
    row_in = pl.BlockSpec((tr, ncol), rmap)
    row_out = pl.BlockSpec((tr, mix), rmap)
    full = lambda x: pl.BlockSpec(x.shape, cmap)
    consts = [mu, w0, a0, kkw, ka, w2p, a2p, g2p]
    args = ([p] if seq else [p, prev]) + consts
    in_specs = ([row_in] if seq else [row_in, row_in]) + [full(c) for c in consts]
    return pl.pallas_call(
        functools.partial(_rwkv_prep_kernel, mix=mix, seq=seq),
        out_shape=tuple(jax.ShapeDtypeStruct((rows, mix), F32) for _ in range(7)),
        grid=grid,
        in_specs=in_specs,
        out_specs=tuple(row_out for _ in range(7)),
        scratch_shapes=[pltpu.VMEM((1, ncol), F32)] if seq else [],
        compiler_params=_cparams(*sem),
        name="rwkv_prep",
    )(*args)


def _rwkv_scan_kernel(r_ref, k_ref, v_ref, kk_ref, kka_ref, w_ref, g_ref, rk_ref, lg_ref, lbias_ref,
                      mix_ref, st_ref, s_ref, *, tb, c):
    n2 = 2 * c
    assert n2 == LANES
    t = pl.program_id(2)

    @pl.when(t == 0)
    def _():
        s_ref[...] = jnp.zeros_like(s_ref)

    ri = _iota2((n2, n2), 0)
    ci = _iota2((n2, n2), 1)
    sh = c.bit_length() - 1
    same_head = jnp.right_shift(ri, sh) == jnp.right_shift(ci, sh)
    strict = same_head & (ri > ci)
    incl = same_head & (ri >= ci)
    same16 = jnp.right_shift(ri, 4) == jnp.right_shift(ci, 4)
    eye = (ri == ci).astype(F32)
    tril = (_iota2((c, c), 0) >= _iota2((c, c), 1)).astype(BF16)
    head0 = _iota2((c, LANES), 1) < B_HEAD_DIM

    def stack(x):
        return jnp.concatenate([jnp.where(head0, x, 0.0), jnp.where(head0, 0.0, x)], axis=0)

    for ch in range(tb // c):
        rows = slice(ch * c, (ch + 1) * c)
        r, k, v, kk, kka, w = (x[rows, :] for x in (r_ref, k_ref, v_ref, kk_ref, kka_ref, w_ref))
        lw = jnp.log(w)
        cs = _exact_ldot(tril, lw)
        gend = jnp.exp(cs[c - 1:c, :])
        ginv = jnp.exp(-cs)
        ph = kka * ginv
        kh = k * ginv
        kk_m = stack(kk * jnp.exp(cs - lw))
        r_m = stack(r * jnp.exp(cs))
        p_m = stack(ph)
        k_m = stack(kh)
        v_m = stack(v)
        a_mat = jnp.where(strict, _bdot_nt(kk_m, p_m), 0.0)
        b_mat = jnp.where(strict, _bdot_nt(kk_m, k_m), 0.0)
        rp = jnp.where(incl, _bdot_nt(r_m, p_m), 0.0)
        rk = jnp.where(incl, _bdot_nt(r_m, k_m), 0.0)
        a_d = jnp.where(same16, a_mat, 0.0)
        a_o = a_mat - a_d
        a2 = _bdot(a_d, a_d)
        a4 = _bdot(a2, a2)
        a8 = _bdot(a4, a4)
        t_d = _bdot(_bdot(_bdot(eye - a_d, eye + a2), eye + a4), eye + a8)
        x = _bdot(t_d, a_o)
        t_m = _bdot(_bdot(eye - x, eye + _bdot(x, x)), t_d)
        s = s_ref[...]
        u = _bdot(t_m, _bdot_nt(kk_m, s) + _bdot(b_mat, v_m))
        o2 = _bdot_nt(r_m, s) - _bdot(rp, u) + _bdot(rk, v_m)
        o = o2[0:c, :] + o2[c:n2, :]
        s_ref[...] = s * gend + _bdot_tn(-u, stack(ph * gend)) + _bdot_tn(v_m, stack(kh * gend))
        hd = float(B_HEAD_DIM)
        d = o - _seg_sum(o, B_HEAD_DIM) / hd
        on = d * lax.rsqrt(_seg_sum(d * d, B_HEAD_DIM) / hd + GN_EPS) * lg_ref[...] + lbias_ref[...]
        bonus = _seg_sum(r * k * rk_ref[...], B_HEAD_DIM) * v
        mix_ref[rows, :] = ((on + bonus) * g_ref[rows, :]).astype(mix_ref.dtype)

    @pl.when(t == pl.num_programs(2) - 1)
    def _():
        s = s_ref[...]
        st_ref[0] = s[0:c, 0:c]
        st_ref[1] = s[c:n2, c:n2]


def _rwkv_scan(prep, rk, lg, lbias, nb, t):
    mix = rk.shape[-1]
    hd = B_HEAD_DIM
    nh = mix // hd
    tb = _tile(t, RW_TB, RW_C)
    nt = t // tb
    col = pl.BlockSpec((tb, LANES), lambda b, h, i: (b * nt + i, h))
    vec = pl.BlockSpec((1, LANES), lambda b, h, i: (0, h))
    return pl.pallas_call(
        functools.partial(_rwkv_scan_kernel, tb=tb, c=RW_C),
        out_shape=(jax.ShapeDtypeStruct((nb * t, mix), BF16),
                   jax.ShapeDtypeStruct((nb, nh, hd, hd), F32)),
        grid=(nb, nh // 2, nt),
        in_specs=[col] * 7 + [vec] * 3,
        out_specs=(col, pl.BlockSpec((None, 2, hd, hd), lambda b, h, i: (b, h, 0, 0))),
        scratch_shapes=[pltpu.VMEM((LANES, LANES), F32)],
        compiler_params=_cparams("parallel", "parallel", "arbitrary"),
        name="rwkv_scan",
    )(*prep, rk.reshape(1, mix), lg.reshape(1, mix), lbias.reshape(1, mix))


def _rwkv_sample_kernel(r_ref, k_ref, v_ref, kk_ref, kka_ref, w_ref, g_ref, rk_ref, lg_ref, lbias_ref,
                        s_ref, o_ref, so_ref, *, bb):
    hd = s_ref.shape[-1]
    eye = (_iota2((hd, hd), 0) == _iota2((hd, hd), 1)).astype(F32)

    def body(b, carry):
        r, k, v, kk, kka, w, g = (x[b] for x in (r_ref, k_ref, v_ref, kk_ref, kka_ref, w_ref, g_ref))
        s = s_ref[b]
        sa = jnp.sum(s * (-kk), axis=-1, keepdims=True)
        vcol = jnp.sum(eye[None] * v, axis=-1, keepdims=True)
        sn = s * w + sa * kka + vcol * k
        so_ref[b] = sn
        ocol = jnp.sum(sn * r, axis=-1, keepdims=True)
        o = jnp.sum(eye[None] * ocol, axis=1, keepdims=True)
        d = o - jnp.mean(o, axis=-1, keepdims=True)
        on = d * lax.rsqrt(jnp.mean(d * d, axis=-1, keepdims=True) + GN_EPS) * lg_ref[...] + lbias_ref[...]
        bonus = jnp.sum(r * k * rk_ref[...], axis=-1, keepdims=True) * v
        o_ref[b] = (on + bonus) * g
        return carry

    lax.fori_loop(0, bb, body, 0)


def _rwkv_sample(prep, rk, lg, lbias, states, layer):
    mix = rk.shape[-1]
    hd = B_HEAD_DIM
    nh = mix // hd
    nb = prep[0].shape[0]
    bb = 8
    row = pl.BlockSpec((bb, nh, 1, hd), lambda i: (i, 0, 0, 0))
    vec = pl.BlockSpec((nh, 1, hd), lambda i: (0, 0, 0))
    st = pl.BlockSpec((bb, nh, hd, hd), lambda i: (i, 0, 0, 0))
    out, new_states = pl.pallas_call(
        functools.partial(_rwkv_sample_kernel, bb=bb),
        out_shape=(jax.ShapeDtypeStruct((nb, nh, 1, hd), F32),
                   jax.ShapeDtypeStruct((nb, nh, hd, hd), F32)),
        grid=(nb // bb,),
        in_specs=[row] * 7 + [vec] * 3 + [pl.BlockSpec((None, bb, nh, hd, hd), lambda i: (layer, i, 0, 0, 0))],
        out_specs=(row, st),
        compiler_params=_cparams("parallel"),
        name="rwkv_sample",
    )(*(x.reshape(nb, nh, 1, hd) for x in prep),
      rk.reshape(nh, 1, hd), lg.reshape(nh, 1, hd), lbias.reshape(nh, 1, hd), states)
    return out.reshape(nb, mix), new_states


def kernel(x_prompt, x_sample, cache_mem_k, cache_mem_v, state_hgrn, state_rwkv, state_rwkv_shift,
           mem_prompt, attn_norm_g, mlp_norm_g, final_norm_g, mem_norm_g, wk_mem, wv_mem,
           a_w_in, a_w_out, a_lb_logits, a_onorm_g,
           b_w_in, b_w_out, b_mu, b_w0, b_w2, b_a0, b_a2, b_g2, b_k_k, b_k_a, b_r_k, b_lnx_g, b_lnx_b,
           mlp_w1, mlp_w2):
    nb, t, d = x_prompt.shape
    ns = x_sample.shape[0]
    depth = attn_norm_g.shape[0]
    mix = a_onorm_g.shape[-1]
    xdim = d - mix
    n_mem = mem_prompt.shape[1]
    b_cols = b_mu.shape[-1]
    b_pad = -(-b_cols // LANES) * LANES
    lora0 = 3 * mix

    lbs = _lower_bounds(a_lb_logits)

    m = _rmsnorm(mem_prompt.reshape(nb * n_mem, d), mem_norm_g, BF16)
    mem_k = [_matmul(m, wk_mem, layer=l, name="mem_k") for l in range(depth)]
    mem_v = [_matmul(m, wv_mem, layer=l, name="mem_v") for l in range(depth)]
    cache_k = cache_mem_k.reshape(depth, ns, n_mem, xdim)
    cache_v = cache_mem_v.reshape(depth, ns, n_mem, xdim)

    xp = x_prompt.reshape(nb * t, d)
    xs = x_sample.reshape(ns, d)
    hgrn_p, hgrn_s, rwkv_p, rwkv_s, shift_p, shift_s = [], [], [], [], [], []
    for layer in range(depth):
        j = layer // 2
        g_attn = attn_norm_g[layer]
        hp = _rmsnorm(xp, g_attn, BF16)
        hs = _rmsnorm(xs, g_attn, F32)
        if layer % 2 == 0:
            pp = _matmul(hp, a_w_in, layer=j, name="a_in_prompt")
            ps = _matmul(hs, a_w_in, layer=j, name="a_in_sample")
            mix_p, st_p = _hgrn_prompt(pp, lbs[j], a_onorm_g[j], nb, t)
            mix_s, st_s = _hgrn_sample(ps, lbs[j], a_onorm_g[j], state_hgrn, j)
            hgrn_p.append(st_p)
            hgrn_s.append(st_s)
            qp, qs, q_blk = pp, ps, (4 * mix) // xdim
            w_out = a_w_out
        else:
            zeros = lambda n: jnp.zeros((n, mix), F32)
            params = (
                jnp.pad(b_mu[j], (0, b_pad - b_cols)).reshape(1, b_pad),
                b_w0[j].reshape(1, mix), b_a0[j].reshape(1, mix),
                b_k_k[j].reshape(1, mix), b_k_a[j].reshape(1, mix),
                jnp.concatenate([b_w2[j], zeros(LANES - B_DECAY_LORA)], axis=0),
                jnp.concatenate([zeros(B_DECAY_LORA), b_a2[j]], axis=0),
                jnp.concatenate([b_g2[j], zeros(2 * LANES - B_GATE_LORA)], axis=0),
            )
            assert lora0 + B_DECAY_LORA + B_AAA_LORA + B_GATE_LORA == b_cols
            w_q = b_w_in[j][:, b_cols:]
            pp = _matmul(hp, b_w_in, layer=j, n_out=b_pad, name="b_in_prompt")
            ps = _matmul(hs, b_w_in, layer=j, n_out=b_pad, name="b_in_sample")
            ps_prev = _matmul(state_rwkv_shift[j], b_w_in, layer=j, n_out=b_pad, name="b_in_shift")
            qp = _matmul(hp, w_q, name="b_q_prompt")
            qs = _matmul(hs, w_q, name="b_q_sample")
            q_blk = 0
            rk, lg, lbias = b_r_k[j].reshape(mix), b_lnx_g[j], b_lnx_b[j]
            mix_p, st_p = _rwkv_scan(_rwkv_prep(pp, None, params, nb, t), rk, lg, lbias, nb, t)
            mix_s, st_s = _rwkv_sample(_rwkv_prep(ps, ps_prev, params, ns, 1), rk, lg, lbias,
                                       state_rwkv, j)
            rwkv_p.append(st_p)
            rwkv_s.append(st_s)
            last = x_prompt[:, t - 1, :] if layer == 0 else xp.reshape(nb, t, d)[:, t - 1, :]
            shift_p.append(_rmsnorm(last, g_attn, F32))
            shift_s.append(hs)
            w_out = b_w_out
        xo_p = _xattn_prompt(qp, q_blk, mem_k[layer].reshape(nb, n_mem, xdim),
                             mem_v[layer].reshape(nb, n_mem, xdim), nb, t)
        xo_s = _xattn_sample(qs, q_blk, cache_k, cache_v, layer)
        xp = _matmul(mix_p, w_out, layer=j, a2=xo_p, res=xp, name="out_prompt")
        xs = _matmul(mix_s, w_out, layer=j, a2=xo_s, res=xs, name="out_sample")
        g_mlp = mlp_norm_g[layer]
        up = _matmul(_rmsnorm(xp, g_mlp, BF16), mlp_w1, layer=layer, act="relu2", out_dtype=BF16,
                     name="mlp1_prompt")
        xp = _matmul(up, mlp_w2, layer=layer, res=xp, name="mlp2_prompt")
        us = _matmul(_rmsnorm(xs, g_mlp, F32), mlp_w1, layer=layer, act="relu2", name="mlp1_sample")
        xs = _matmul(us, mlp_w2, layer=layer, res=xs, name="mlp2_sample")

    y_p = _rmsnorm(xp, final_norm_g, F32).reshape(nb, t, d)
    y_s = _rmsnorm(xs, final_norm_g, F32).reshape(ns, 1, d)
    kv_shape = (depth, nb, n_mem, X_HEADS, X_HEAD_DIM)
    return (y_p, y_s, jnp.stack(mem_k).reshape(kv_shape), jnp.stack(mem_v).reshape(kv_shape),
            jnp.stack(hgrn_p), jnp.stack(rwkv_p), jnp.stack(shift_p),
            jnp.stack(hgrn_s), jnp.stack(rwkv_s), jnp.stack(shift_s))
```

```python
import functools

import jax
import jax.numpy as jnp
from jax import lax
from jax.experimental import pallas as pl
from jax.experimental.pallas import tpu as pltpu

F32 = jnp.float32
BF16 = jnp.bfloat16

RMS_EPS = 1e-6
GN_EPS = 64e-5
F_FLOOR = 1e-30
X_HEADS = 4
X_HEAD_DIM = 128
A_HEAD_DIM = 128
B_HEAD_DIM = 64
B_DECAY_LORA = 64
B_AAA_LORA = 64
B_GATE_LORA = 224

LANES = 128
VMEM_LIMIT = 56 * 1024 * 1024


def _cparams(*sem):
    return pltpu.CompilerParams(dimension_semantics=sem, vmem_limit_bytes=VMEM_LIMIT)


def _tile(n, cap, mult=8):
    if n <= cap:
        return n
    best = None
    for d in range(mult, cap + 1, mult):
        if n % d == 0:
            best = d
    assert best is not None, (n, cap, mult)
    return best


def _bdot(a, b):
    return jnp.dot(a.astype(BF16), b.astype(BF16), preferred_element_type=F32)


def _bdot_nt(a, b):
    return lax.dot_general(a.astype(BF16), b.astype(BF16), (((1,), (1,)), ((), ())),
                           preferred_element_type=F32)


def _bdot_tn(a, b):
    return lax.dot_general(a.astype(BF16), b.astype(BF16), (((0,), (0,)), ((), ())),
                           preferred_element_type=F32)


def _split3(x):
    hi = x.astype(BF16)
    r1 = x - hi.astype(F32)
    mid = r1.astype(BF16)
    lo = (r1 - mid.astype(F32)).astype(BF16)
    return hi, mid, lo


def _exact_ldot(m01, x):
    hi, mid, lo = _split3(x)
    d = lambda p: jnp.dot(m01, p, preferred_element_type=F32)
    return d(hi) + d(mid) + d(lo)


def _sigmoid(x):
    return 1.0 / (1.0 + jnp.exp(-x))


def _silu(x):
    return x * _sigmoid(x)


def _iota2(shape, axis):
    return lax.broadcasted_iota(jnp.int32, shape, axis)


def _rmsnorm_kernel(x_ref, g_ref, o_ref):
    x = x_ref[...]
    y = x * lax.rsqrt(jnp.mean(x * x, axis=-1, keepdims=True) + RMS_EPS)
    o_ref[...] = (y * g_ref[...]).astype(o_ref.dtype)


def _rmsnorm(x, g, out_dtype):
    m, d = x.shape
    tm = _tile(m, 512, 16)
    return pl.pallas_call(
        _rmsnorm_kernel,
        out_shape=jax.ShapeDtypeStruct((m, d), out_dtype),
        grid=(m // tm,),
        in_specs=[pl.BlockSpec((tm, d), lambda i: (i, 0)),
                  pl.BlockSpec((1, d), lambda i: (0, 0))],
        out_specs=pl.BlockSpec((tm, d), lambda i: (i, 0)),
        compiler_params=_cparams("parallel"),
        name="rmsnorm",
    )(x, g.reshape(1, d))


def _matmul_kernel(*refs, k1, nk, act, has_res, has_a2, wt):
    it = iter(refs)
    a1_ref = next(it)
    a2_ref = next(it) if has_a2 else None
    w_ref = next(it)
    res_ref = next(it) if has_res else None
    o_ref = next(it)

    if nk == 1:
        if wt:
            assert not has_a2
            acc = _bdot_nt(a1_ref[...], w_ref[...])
        else:
            acc = _bdot(a1_ref[...], w_ref[0:k1, :])
        if has_a2:
            acc = acc + _bdot(a2_ref[...], w_ref[k1:, :])
        if act == "relu2":
            acc = jnp.square(jnp.maximum(acc, 0.0))
        if has_res:
            acc = res_ref[...] + acc
        o_ref[...] = acc.astype(o_ref.dtype)
        return

    k = pl.program_id(2)
    part = _bdot(a1_ref[...], w_ref[...])

    @pl.when(k == 0)
    def _():
        o_ref[...] = (res_ref[...] + part) if has_res else part

    @pl.when(k > 0)
    def _():
        o_ref[...] += part


def _matmul(a1, w, *, layer=None, a2=None, res=None, act=None, out_dtype=F32, n_out=None, wt=False,
            tm_cap=2048, tn_cap=512, tk_cap=2048, name="matmul"):
    m, k1 = a1.shape
    k2 = a2.shape[1] if a2 is not None else 0
    kdim = k1 + k2
    n = w.shape[-2 if wt else -1] if n_out is None else n_out
    tm = _tile(m, tm_cap, 16 if a1.dtype == BF16 or out_dtype == BF16 else 8)
    tn = _tile(n, tn_cap, LANES)
    tk = _tile(kdim, tk_cap, LANES)
    nk = kdim // tk
    if nk > 1:
        assert a2 is None and act is None and out_dtype == F32 and not wt
    assert w.shape[-1 if wt else -2] == kdim
    w_block = (tn, tk) if wt else (tk, tn)
    w_idx = (lambda k, j: (j, k)) if wt else (lambda k, j: (k, j))
    if layer is None:
        w_spec = pl.BlockSpec(w_block, lambda i, j, k: w_idx(k, j))
    else:
        w_spec = pl.BlockSpec((None,) + w_block, lambda i, j, k: (layer,) + w_idx(k, j))
    in_specs = [pl.BlockSpec((tm, min(tk, k1)), lambda i, j, k: (i, k))]
    args = [a1]
    if a2 is not None:
        in_specs.append(pl.BlockSpec((tm, k2), lambda i, j, k: (i, 0)))
        args.append(a2)
    in_specs.append(w_spec)
    args.append(w)
    if res is not None:
        in_specs.append(pl.BlockSpec((tm, tn), lambda i, j, k: (i, j)))
        args.append(res)
    kern = functools.partial(_matmul_kernel, k1=k1, nk=nk, act=act,
                             has_res=res is not None, has_a2=a2 is not None, wt=wt)
    return pl.pallas_call(
        kern,
        out_shape=jax.ShapeDtypeStruct((m, n), out_dtype),
        grid=(m // tm, n // tn, nk),
        in_specs=in_specs,
        out_specs=pl.BlockSpec((tm, tn), lambda i, j, k: (i, j)),
        compiler_params=_cparams("parallel", "parallel", "arbitrary"),
        name=name,
    )(*args)


def _lower_bounds_kernel(lg_ref, o_ref):
    lg = lg_ref[...]
    e = jnp.exp(lg - jnp.max(lg, axis=0, keepdims=True))
    p = e / jnp.sum(e, axis=0, keepdims=True)
    run = jnp.zeros_like(p[0:1])
    for l in range(lg.shape[0]):
        run = run + p[l:l + 1]
        o_ref[l:l + 1, :] = run - p[0:1]


def _lower_bounds(logits):
    return pl.pallas_call(
        _lower_bounds_kernel,
        out_shape=jax.ShapeDtypeStruct(logits.shape, F32),
        name="hgrn_lower_bounds",
    )(logits)


def _softmax_rows(s):
    e = jnp.exp(s - jnp.max(s, axis=-1, keepdims=True))
    return e / jnp.sum(e, axis=-1, keepdims=True)


def _xattn_prompt_kernel(q_ref, k_ref, v_ref, o_ref):
    scale = X_HEAD_DIM ** -0.5
    for h in range(X_HEADS):
        sl = slice(h * X_HEAD_DIM, (h + 1) * X_HEAD_DIM)
        p = _softmax_rows(_bdot_nt(q_ref[:, sl], k_ref[:, sl]) * scale)
        o_ref[:, sl] = _bdot(p, v_ref[:, sl]).astype(o_ref.dtype)


def _xattn_prompt(q_arr, q_blk, mem_k, mem_v, nb, t):
    xdim = mem_k.shape[-1]
    n_mem = mem_k.shape[1]
    tq = _tile(t, 512, 16)
    nt = t // tq
    return pl.pallas_call(
        _xattn_prompt_kernel,
        out_shape=jax.ShapeDtypeStruct((nb * t, xdim), BF16),
        grid=(nb, nt),
        in_specs=[pl.BlockSpec((tq, xdim), lambda b, i: (b * nt + i, q_blk)),
                  pl.BlockSpec((None, n_mem, xdim), lambda b, i: (b, 0, 0)),
                  pl.BlockSpec((None, n_mem, xdim), lambda b, i: (b, 0, 0))],
        out_specs=pl.BlockSpec((tq, xdim), lambda b, i: (b * nt + i, 0)),
        compiler_params=_cparams("parallel", "parallel"),
        name="xattn_prompt",
    )(q_arr, mem_k, mem_v)


def _xattn_sample_kernel(q_ref, k_ref, v_ref, o_ref, *, bb):
    scale = X_HEAD_DIM ** -0.5
    nh = X_HEADS

    def fold(x):
        return x + pltpu.roll(x, nh, axis=0)

    def body(b, carry):
        q8 = q_ref[b] * scale
        s = jnp.sum(k_ref[b] * q8[None], axis=-1, keepdims=True)
        mx = jnp.broadcast_to(jnp.max(s, axis=0), (2 * nh, X_HEAD_DIM))
        mx = jnp.maximum(mx, pltpu.roll(mx, nh, axis=0))
        e = jnp.exp(s - mx[None])
        num = fold(jnp.sum(e * v_ref[b], axis=0))
        o_ref[b] = (num / fold(jnp.sum(e, axis=0)))[0:nh]
        return carry

    lax.fori_loop(0, bb, body, 0)


def _xattn_sample(q, cache_k, cache_v, layer):
    depth, nb, n_mem, nh, hd = cache_k.shape
    assert nh == X_HEADS and hd == X_HEAD_DIM and (n_mem * nh) % 8 == 0
    g = n_mem * nh // 8
    bb = 8
    q4 = q.reshape(nb, nh, hd)
    q8 = jnp.concatenate([q4, q4], axis=1)
    kv_spec = pl.BlockSpec((None, bb, g, 8, hd), lambda i: (layer, i, 0, 0, 0))
    out = pl.pallas_call(
        functools.partial(_xattn_sample_kernel, bb=bb),
        out_shape=jax.ShapeDtypeStruct((nb, nh, hd), F32),
        grid=(nb // bb,),
        in_specs=[pl.BlockSpec((bb, 2 * nh, hd), lambda i: (i, 0, 0)), kv_spec, kv_spec],
        out_specs=pl.BlockSpec((bb, nh, hd), lambda i: (i, 0, 0)),
        compiler_params=_cparams("parallel"),
        name="xattn_sample",
    )(q8, cache_k.reshape(depth, nb, g, 8, hd), cache_v.reshape(depth, nb, g, 8, hd))
    return out.reshape(nb, nh * hd)


HG_TB = 256
HG_C = 16


def _hgrn_prompt_kernel(q_ref, f_ref, i_ref, g_ref, lb_ref, og_ref, mix_ref, st_ref,
                        s_ref, qs_ref, kk_ref, bl_ref, o_ref, *, tb, c):
    t = pl.program_id(2)

    @pl.when(t == 0)
    def _():
        s_ref[...] = jnp.zeros_like(s_ref)

    lb = lb_ref[...]
    ff = f_ref[...]
    lg = jnp.log(jnp.maximum(lb + (1.0 - lb) * _sigmoid(ff), F_FLOOR))
    kk_ref[...] = (1.0 - lb) * _sigmoid(-ff)
    qs_ref[...] = _silu(q_ref[...])
    shift = c.bit_length() - 1
    ri = _iota2((tb, tb), 0)
    ci = _iota2((tb, tb), 1)
    same_chunk = jnp.right_shift(ri, shift) == jnp.right_shift(ci, shift)
    lmat = jnp.where(same_chunk & (ci <= ri), 1.0, 0.0).astype(BF16)
    bl_ref[...] = _exact_ldot(lmat, lg)
    eye = (_iota2((LANES, LANES), 0) == _iota2((LANES, LANES), 1)).astype(F32)
    rowi = _iota2((c, 1), 0)

    nch = tb // c
    upd, ecol = [], []
    for idx in range(nch):
        rows = slice(idx * c, (idx + 1) * c)
        bend = bl_ref[(idx + 1) * c - 1:(idx + 1) * c, :]
        upd.append(_bdot_tn(kk_ref[rows, :] * jnp.exp(bend - bl_ref[rows, :]), i_ref[rows, :]))
        ecol.append(jnp.sum(eye * jnp.exp(bend), axis=-1, keepdims=True))
    states = [s_ref[...]]
    for idx in range(nch):
        states.append(states[idx] * ecol[idx] + upd[idx])
    s_ref[...] = states[nch]
    for idx in range(nch):
        r0 = idx * c
        rows = slice(r0, r0 + c)
        blc = bl_ref[rows, :]
        qc = qs_ref[rows, :]
        o = _bdot(qc * jnp.exp(blc), states[idx])
        for j in range(c):
            r = r0 + j
            d = jnp.exp(jnp.minimum(blc - bl_ref[r:r + 1, :], 0.0))
            a = jnp.sum(qc * kk_ref[r:r + 1, :] * d, axis=-1, keepdims=True)
            o = o + jnp.where(rowi >= j, a, 0.0) * i_ref[r:r + 1, :]
        o_ref[rows, :] = o
    o = o_ref[...]
    y = o * lax.rsqrt(jnp.mean(o * o, axis=-1, keepdims=True) + RMS_EPS) * og_ref[...]
    mix_ref[...] = (y * _silu(g_ref[...])).astype(mix_ref.dtype)

    @pl.when(t == pl.num_programs(2) - 1)
    def _():
        st_ref[...] = s_ref[...]


def _hgrn_prompt(p, lb, og, nb, t):
    mix = lb.shape[0]
    hd = A_HEAD_DIM
    nh = mix // hd
    tb = _tile(t, HG_TB, HG_C)
    nt = t // tb
    col = lambda off: pl.BlockSpec((tb, hd), lambda b, h, i: (b * nt + i, off + h))
    vec = pl.BlockSpec((1, hd), lambda b, h, i: (0, h))
    return pl.pallas_call(
        functools.partial(_hgrn_prompt_kernel, tb=tb, c=HG_C),
        out_shape=(jax.ShapeDtypeStruct((nb * t, mix), BF16),
                   jax.ShapeDtypeStruct((nb, nh, hd, hd), F32)),
        grid=(nb, nh, nt),
        in_specs=[col(0), col(nh), col(2 * nh), col(3 * nh), vec, vec],
        out_specs=(pl.BlockSpec((tb, hd), lambda b, h, i: (b * nt + i, h)),
                   pl.BlockSpec((None, None, hd, hd), lambda b, h, i: (b, h, 0, 0))),
        scratch_shapes=[pltpu.VMEM((hd, hd), F32)] + [pltpu.VMEM((tb, hd), F32)] * 4,
        compiler_params=_cparams("parallel", "parallel", "arbitrary"),
        name="hgrn_prompt",
    )(p, p, p, p, lb.reshape(1, mix), og.reshape(1, mix))


def _hgrn_sample_kernel(p_ref, lb_ref, og_ref, s_ref, *rest, bb):
    mix_ref, so_ref = rest[-2:]
    hd = s_ref.shape[-1]
    eye = (_iota2((hd, hd), 0) == _iota2((hd, hd), 1)).astype(F32)
    lb = lb_ref[...]
    col = lambda r: jnp.sum(eye[None] * r, axis=-1, keepdims=True)

    def body(b, carry):
        q, f, iv, g = p_ref[b, 0], p_ref[b, 1], p_ref[b, 2], p_ref[b, 3]
        dec = jnp.maximum(lb + (1.0 - lb) * _sigmoid(f), F_FLOOR)
        k = (1.0 - lb) * _sigmoid(-f)
        sn = s_ref[b] * col(dec) + col(k) * iv
        so_ref[b] = sn
        o = jnp.sum(sn * col(_silu(q)), axis=1, keepdims=True)
        y = o * lax.rsqrt(jnp.mean(o * o, axis=-1, keepdims=True) + RMS_EPS) * og_ref[...]
        mix_ref[b] = y * _silu(g)
        return carry

    lax.fori_loop(0, bb, body, 0)


def _stacked_state_io(states_shape, block, index_map, stacked):
    out_shape = jax.ShapeDtypeStruct(states_shape, F32)
    out_spec = pl.BlockSpec(block, index_map)
    if stacked is None:
        return out_shape, out_spec, [], []
    return out_shape, out_spec, [pl.BlockSpec(memory_space=pl.ANY)], [stacked]


def _hgrn_sample(p, lb, og, states, layer, stacked):
    mix = lb.shape[0]
    hd = A_HEAD_DIM
    nh = mix // hd
    nb = p.shape[0]
    bb = 4
    p5 = p[:, :4 * mix].reshape(nb, 4, nh, 1, hd)
    vec = pl.BlockSpec((nh, 1, hd), lambda i: (0, 0, 0))
    st_block = (None, bb, nh, hd, hd)
    st_map = lambda i: (layer, i, 0, 0, 0)
    so_shape, so_spec, alias_specs, alias_args = _stacked_state_io(states.shape, st_block, st_map, stacked)
    out, new_states = pl.pallas_call(
        functools.partial(_hgrn_sample_kernel, bb=bb),
        out_shape=(jax.ShapeDtypeStruct((nb, nh, 1, hd), F32), so_shape),
        grid=(nb // bb,),
        in_specs=[pl.BlockSpec((bb, 4, nh, 1, hd), lambda i: (i, 0, 0, 0, 0)), vec, vec,
                  pl.BlockSpec(st_block, st_map)] + alias_specs,
        out_specs=(pl.BlockSpec((bb, nh, 1, hd), lambda i: (i, 0, 0, 0)), so_spec),
        input_output_aliases={4: 1} if alias_args else {},
        compiler_params=_cparams("parallel"),
        name="hgrn_sample",
    )(p5, lb.reshape(nh, 1, hd), og.reshape(nh, 1, hd), states, *alias_args)
    return out.reshape(nb, mix), new_states


RW_TR = 128
RW_TB = 256
RW_C = 64
RW_PAIRS = 3


def _softplus(x):
    return jnp.maximum(x, 0.0) + jnp.log(1.0 + jnp.exp(-jnp.abs(x)))


def _seg_sum(x, seg):
    lo = _iota2(x.shape, x.ndim - 1) < seg
    s0 = jnp.sum(jnp.where(lo, x, 0.0), axis=-1, keepdims=True)
    s1 = jnp.sum(jnp.where(lo, 0.0, x), axis=-1, keepdims=True)
    return jnp.where(lo, s0, s1)


def _rwkv_prep_kernel(*refs, mix, seq):
    if seq:
        (p_ref, mu_ref, w0_ref, a0_ref, kkw_ref, ka_ref, w2_ref, a2_ref, g2_ref,
         r_o, k_o, v_o, kk_o, kka_o, w_o, g_o, carry_ref) = refs
    else:
        (p_ref, prev_ref, mu_ref, w0_ref, a0_ref, kkw_ref, ka_ref, w2_ref, a2_ref, g2_ref,
         r_o, k_o, v_o, kk_o, kka_o, w_o, g_o) = refs
    p = p_ref[...]
    tr = p.shape[0]
    if seq:
        @pl.when(pl.program_id(1) == 0)
        def _():
            carry_ref[...] = jnp.zeros_like(carry_ref)

        prev = jnp.where(_iota2(p.shape, 0) == 0, carry_ref[...], pltpu.roll(p, 1, axis=0))
        carry_ref[...] = p[tr - 1:tr, :]
    else:
        prev = prev_ref[...]
    xs = p + (prev - p) * mu_ref[...]
    r = xs[:, :mix]
    k = xs[:, mix:2 * mix]
    v = xs[:, 2 * mix:3 * mix]
    la = xs[:, 3 * mix:3 * mix + LANES]
    gl = xs[:, 3 * mix + LANES:3 * mix + 3 * LANES]
    w_raw = -_softplus(-(w0_ref[...] + _bdot(jnp.tanh(la), w2_ref[...]))) - 0.5
    a = _sigmoid(a0_ref[...] + _bdot(la, a2_ref[...]))
    r_o[...] = r
    v_o[...] = v
    w_o[...] = jnp.exp(-jnp.exp(w_raw))
    g_o[...] = _bdot(_sigmoid(gl), g2_ref[...])
    k_o[...] = k * (1.0 + (a - 1.0) * ka_ref[...])
    for j in range(mix // LANES):
        sl = slice(j * LANES, (j + 1) * LANES)
        kk = k[:, sl] * kkw_ref[:, sl]
        kk = kk / jnp.maximum(jnp.sqrt(_seg_sum(kk * kk, B_HEAD_DIM)), 1e-12)
        kk_o[:, sl] = kk
        kka_o[:, sl] = kk * a[:, sl]


def _rwkv_prep(p, prev, params, nb, t):
    mu, w0, a0, kkw, ka, w2p, a2p, g2p = params
    mix = w0.shape[-1]
    rows, ncol = p.shape
    seq = prev is None
    tr = _tile(t, RW_TR, 8) if seq else _tile(rows, RW_TR, 8)
    nt = t // tr if seq else rows // tr
    if seq:
        grid = (nb, nt)
        rmap = lambda b, i: (b * nt + i, 0)
        cmap = lambda b, i: (0, 0)
        sem = ("parallel", "arbitrary")
    else:
        grid = (nt,)
        rmap = lambda i: (i, 0)
        cmap = lambda i: (0, 0)
        sem = ("parallel",)
    row_in = pl.BlockSpec((tr, ncol), rmap)
    row_out = pl.BlockSpec((tr, mix), rmap)
    full = lambda x: pl.BlockSpec(x.shape, cmap)
    consts = [mu, w0, a0, kkw, ka, w2p, a2p, g2p]
    args = ([p] if seq else [p, prev]) + consts
    in_specs = ([row_in] if seq else [row_in, row_in]) + [full(c) for c in consts]
    return pl.pallas_call(
        functools.partial(_rwkv_prep_kernel, mix=mix, seq=seq),
        out_shape=tuple(jax.ShapeDtypeStruct((rows, mix), F32) for _ in range(7)),
        grid=grid,
        in_specs=in_specs,
        out_specs=tuple(row_out for _ in range(7)),
        scratch_shapes=[pltpu.VMEM((1, ncol), F32)] if seq else [],
        compiler_params=_cparams(*sem),
        name="rwkv_prep",
    )(*args)


def _rwkv_scan_kernel(r_ref, k_ref, v_ref, kk_ref, kka_ref, w_ref, g_ref, rk_ref, lg_ref, lbias_ref,
                      mix_ref, st_ref, s_ref, *, tb, c, pp):
    n2 = 2 * c
    assert n2 == LANES
    t = pl.program_id(2)

    @pl.when(t == 0)
    def _():
        s_ref[...] = jnp.zeros_like(s_ref)

    ri = _iota2((n2, n2), 0)
    ci = _iota2((n2, n2), 1)
    blk = lambda x, n: jnp.right_shift(x, n.bit_length() - 1)
    same_head = blk(ri, c) == blk(ci, c)
    strict = same_head & (ri > ci)
    incl = same_head & (ri >= ci)
    eye = (ri == ci).astype(F32)
    tril = (_iota2((c, c), 0) >= _iota2((c, c), 1)).astype(BF16)
    head0 = _iota2((c, LANES), 1) < B_HEAD_DIM
    hd = float(B_HEAD_DIM)

    def sibling(n):
        return (blk(ri, 2 * n) == blk(ci, 2 * n)) & (blk(ri, n) > blk(ci, n))

    def stack(x):
        return jnp.concatenate([jnp.where(head0, x, 0.0), jnp.where(head0, 0.0, x)], axis=0)

    nch = tb // c
    inst = [(p, ch) for p in range(pp) for ch in range(nch)]
    each = lambda f, *lists: [f(*xs) for xs in zip(*lists)]
    win = lambda ref: [ref[ch * c:(ch + 1) * c, p * LANES:(p + 1) * LANES] for p, ch in inst]
    r, k, v, kk, kka, w = (win(x) for x in (r_ref, k_ref, v_ref, kk_ref, kka_ref, w_ref))
    lw = each(jnp.log, w)
    cs = each(lambda x: _exact_ldot(tril, x), lw)
    gend = each(lambda x: jnp.exp(x[c - 1:c, :]), cs)
    ginv = each(lambda x: jnp.exp(-x), cs)
    ph = each(jnp.multiply, kka, ginv)
    kh = each(jnp.multiply, k, ginv)
    kk_m = each(lambda a, b, d: stack(a * jnp.exp(b - d)), kk, cs, lw)
    r_m = each(lambda a, b: stack(a * jnp.exp(b)), r, cs)
    v_m = each(stack, v)
    lhs = each(lambda a, b: jnp.concatenate([a, b], axis=0), kk_m, r_m)
    gp = each(lambda a, b: _bdot_nt(a, stack(b)), lhs, ph)
    gk = each(lambda a, b: _bdot_nt(a, stack(b)), lhs, kh)
    a_mat = each(lambda g: jnp.where(strict, g[0:n2], 0.0), gp)
    b_mat = each(lambda g: jnp.where(strict, g[0:n2], 0.0), gk)
    rp = each(lambda g: jnp.where(incl, g[n2:], 0.0), gp)
    rk = each(lambda g: jnp.where(incl, g[n2:], 0.0), gk)
    bv = each(_bdot, b_mat, v_m)
    rkv = each(_bdot, rk, v_m)
    t_m = each(lambda a: eye - jnp.where(blk(ri, 2) == blk(ci, 2), a, 0.0), a_mat)
    n = 2
    while n < c:
        y = each(lambda a, tm: _bdot(jnp.where(sibling(n), a, 0.0), tm), a_mat, t_m)
        t_m = each(lambda tm, yy: tm - _bdot(tm, yy), t_m, y)
        n *= 2
    ku = each(lambda tm, a, b: _bdot(tm, jnp.concatenate([a, b], axis=1)), t_m, kk_m, bv)
    rpku = each(_bdot, rp, ku)
    ml = each(lambda a, b, g: _bdot_tn(a, stack(b * g)), ku, ph, gend)
    vk = each(lambda a, b, g: _bdot_tn(a, stack(b * g)), v_m, kh, gend)
    r_t = each(lambda a, b: a - b[:, 0:n2], r_m, rpku)
    o_0 = each(lambda a, b: a - b[:, n2:], rkv, rpku)
    cst = each(lambda a, b: a - b[n2:], vk, ml)
    s = [s_ref[p] for p in range(pp)]
    o2 = [None] * len(inst)
    for ch in range(nch):
        for p in range(pp):
            i = p * nch + ch
            o2[i] = _bdot_nt(r_t[i], s[p]) + o_0[i]
            s[p] = s[p] * gend[i] - _bdot(s[p], ml[i][0:n2]) + cst[i]
    for p in range(pp):
        s_ref[p] = s[p]
    for i, (p, ch) in enumerate(inst):
        rows = slice(ch * c, (ch + 1) * c)
        cols = slice(p * LANES, (p + 1) * LANES)
        o = o2[i][0:c, :] + o2[i][c:n2, :]
        d = o - _seg_sum(o, B_HEAD_DIM) / hd
        on = (d * lax.rsqrt(_seg_sum(d * d, B_HEAD_DIM) / hd + GN_EPS) * lg_ref[:, cols]
              + lbias_ref[:, cols])
        bonus = _seg_sum(r[i] * k[i] * rk_ref[:, cols], B_HEAD_DIM) * v[i]
        mix_ref[rows, cols] = ((on + bonus) * g_ref[rows, cols]).astype(mix_ref.dtype)

    @pl.when(t == pl.num_programs(2) - 1)
    def _():
        for p in range(pp):
            s = s_ref[p]
            st_ref[2 * p] = s[0:c, 0:c]
            st_ref[2 * p + 1] = s[c:n2, c:n2]


def _rwkv_scan(prep, rk, lg, lbias, nb, t):
    mix = rk.shape[-1]
    hd = B_HEAD_DIM
    nh = mix // hd
    pp = RW_PAIRS
    assert nh % (2 * pp) == 0
    tb = _tile(t, RW_TB, RW_C)
    nt = t // tb
    col = pl.BlockSpec((tb, pp * LANES), lambda b, h, i: (b * nt + i, h))
    vec = pl.BlockSpec((1, pp * LANES), lambda b, h, i: (0, h))
    return pl.pallas_call(
        functools.partial(_rwkv_scan_kernel, tb=tb, c=RW_C, pp=pp),
        out_shape=(jax.ShapeDtypeStruct((nb * t, mix), BF16),
                   jax.ShapeDtypeStruct((nb, nh, hd, hd), F32)),
        grid=(nb, nh // (2 * pp), nt),
        in_specs=[col] * 7 + [vec] * 3,
        out_specs=(col, pl.BlockSpec((None, 2 * pp, hd, hd), lambda b, h, i: (b, h, 0, 0))),
        scratch_shapes=[pltpu.VMEM((pp, LANES, LANES), F32)],
        compiler_params=_cparams("parallel", "parallel", "arbitrary"),
        name="rwkv_scan",
    )(*prep, rk.reshape(1, mix), lg.reshape(1, mix), lbias.reshape(1, mix))


def _rwkv_sample_kernel(r_ref, k_ref, v_ref, kk_ref, kka_ref, w_ref, g_ref, rk_ref, lg_ref, lbias_ref,
                        s_ref, *rest):
    o_ref, so_ref, vt_ref, ot_ref = rest[-4:]
    hd = s_ref.shape[1]
    r = r_ref[...]
    k = k_ref[...]
    r_t, k_t, w_t, kka_t = r.T, k.T, w_ref[...].T, kka_ref[...].T
    nkk_t = -(kk_ref[...].T)
    vt_ref[...] = v_ref[...].T
    bon_t = (r * k * rk_ref[...]).T
    for h in range(2):
        sl = slice(h * hd, (h + 1) * hd)
        nkk, kka, w, kf, rr = nkk_t[sl], kka_t[sl], w_t[sl], k_t[sl], r_t[sl]

        def body(i, carry, h=h, nkk=nkk, kka=kka, w=w, kf=kf, rr=rr):
            row = pl.ds(h * hd + i, 1)
            sv = s_ref[h, i]
            sa = jnp.sum(sv * nkk, axis=0, keepdims=True)
            sn = sv * w + sa * kka + vt_ref[row, :] * kf
            so_ref[h, i] = sn
            ot_ref[row, :] = jnp.sum(sn * rr, axis=0, keepdims=True)
            return carry

        lax.fori_loop(0, hd, body, 0, unroll=4)
    dn, bonus = [], []
    for h in range(2):
        sl = slice(h * hd, (h + 1) * hd)
        o = ot_ref[sl, :]
        d = o - jnp.mean(o, axis=0, keepdims=True)
        dn.append(d * lax.rsqrt(jnp.mean(d * d, axis=0, keepdims=True) + GN_EPS))
        bonus.append(jnp.sum(bon_t[sl], axis=0, keepdims=True) * vt_ref[sl, :])
    dn = jnp.concatenate(dn, axis=0).T
    bonus = jnp.concatenate(bonus, axis=0).T
    o_ref[...] = (dn * lg_ref[...] + lbias_ref[...] + bonus) * g_ref[...]


def _rwkv_sample(prep, rk, lg, lbias, states_t, layer, stacked):
    mix = rk.shape[-1]
    hd = B_HEAD_DIM
    nb = prep[0].shape[0]
    assert nb == LANES
    col = pl.BlockSpec((nb, LANES), lambda h: (0, h))
    vec = pl.BlockSpec((1, LANES), lambda h: (0, h))
    st_block = (None, 2, hd, hd, nb)
    st_map = lambda h: (layer, h, 0, 0, 0)
    so_shape, so_spec, alias_specs, alias_args = _stacked_state_io(states_t.shape, st_block, st_map, stacked)
    return pl.pallas_call(
        _rwkv_sample_kernel,
        out_shape=(jax.ShapeDtypeStruct((nb, mix), F32), so_shape),
        grid=(mix // LANES,),
        in_specs=[col] * 7 + [vec] * 3 + [pl.BlockSpec(st_block, st_map)] + alias_specs,
        out_specs=(col, so_spec),
        scratch_shapes=[pltpu.VMEM((LANES, nb), F32)] * 2,
        input_output_aliases={11: 1} if alias_args else {},
        compiler_params=_cparams("parallel"),
        name="rwkv_sample",
    )(*prep, rk.reshape(1, mix), lg.reshape(1, mix), lbias.reshape(1, mix), states_t, *alias_args)


def kernel(x_prompt, x_sample, cache_mem_k, cache_mem_v, state_hgrn, state_rwkv, state_rwkv_shift,
           mem_prompt, attn_norm_g, mlp_norm_g, final_norm_g, mem_norm_g, wk_mem, wv_mem,
           a_w_in, a_w_out, a_lb_logits, a_onorm_g,
           b_w_in, b_w_out, b_mu, b_w0, b_w2, b_a0, b_a2, b_g2, b_k_k, b_k_a, b_r_k, b_lnx_g, b_lnx_b,
           mlp_w1, mlp_w2):
    nb, t, d = x_prompt.shape
    ns = x_sample.shape[0]
    depth = attn_norm_g.shape[0]
    mix = a_onorm_g.shape[-1]
    xdim = d - mix
    n_mem = mem_prompt.shape[1]
    b_cols = b_mu.shape[-1]
    b_pad = -(-b_cols // LANES) * LANES
    lora0 = 3 * mix

    lbs = _lower_bounds(a_lb_logits)

    m = _rmsnorm(mem_prompt.reshape(nb * n_mem, d), mem_norm_g, BF16)
    mem_k = [_matmul(m, wk_mem, layer=l, name="mem_k") for l in range(depth)]
    mem_v = [_matmul(m, wv_mem, layer=l, name="mem_v") for l in range(depth)]
    state_rwkv_t = jnp.transpose(state_rwkv, (0, 2, 3, 4, 1))
    b_w_in_t = jnp.transpose(b_w_in, (0, 2, 1))

    xp = x_prompt.reshape(nb * t, d)
    xs = x_sample.reshape(ns, d)
    hgrn_p, rwkv_p, shift_p, shift_s = [], [], [], []
    hgrn_s = rwkv_s_t = None
    for layer in range(depth):
        j = layer // 2
        g_attn = attn_norm_g[layer]
        hp = _rmsnorm(xp, g_attn, BF16)
        hs = _rmsnorm(xs, g_attn, F32)
        if layer % 2 == 0:
            pp = _matmul(hp, a_w_in, layer=j, name="a_in_prompt")
            ps = _matmul(hs, a_w_in, layer=j, name="a_in_sample")
            mix_p, st_p = _hgrn_prompt(pp, lbs[j], a_onorm_g[j], nb, t)
            mix_s, hgrn_s = _hgrn_sample(ps, lbs[j], a_onorm_g[j], state_hgrn, j, hgrn_s)
            hgrn_p.append(st_p)
            qp, q_blk = pp, (4 * mix) // xdim
            qs = ps[:, 4 * mix:]
            w_out = a_w_out
        else:
            zeros = lambda n: jnp.zeros((n, mix), F32)
            params = (
                jnp.pad(b_mu[j], (0, b_pad - b_cols)).reshape(1, b_pad),
                b_w0[j].reshape(1, mix), b_a0[j].reshape(1, mix),
                b_k_k[j].reshape(1, mix), b_k_a[j].reshape(1, mix),
                jnp.concatenate([b_w2[j], zeros(LANES - B_DECAY_LORA)], axis=0),
                jnp.concatenate([zeros(B_DECAY_LORA), b_a2[j]], axis=0),
                jnp.concatenate([b_g2[j], zeros(2 * LANES - B_GATE_LORA)], axis=0),
            )
            assert lora0 + B_DECAY_LORA + B_AAA_LORA + B_GATE_LORA == b_cols
            w_q = lax.slice(b_w_in_t, (j, b_cols, 0), (j + 1, b_w_in_t.shape[1], d)).reshape(xdim, d)
            pp = _matmul(hp, b_w_in_t, layer=j, n_out=b_pad, wt=True, name="b_in_prompt")
            ps = _matmul(hs, b_w_in_t, layer=j, n_out=b_pad, wt=True, name="b_in_sample")
            ps_prev = _matmul(state_rwkv_shift[j], b_w_in_t, layer=j, n_out=b_pad, wt=True,
                              name="b_in_shift")
            qp = _matmul(hp, w_q, wt=True, name="b_q_prompt")
            qs = _matmul(hs, w_q, wt=True, name="b_q_sample")
            q_blk = 0
            rk, lg, lbias = b_r_k[j].reshape(mix), b_lnx_g[j], b_lnx_b[j]
            mix_p, st_p = _rwkv_scan(_rwkv_prep(pp, None, params, nb, t), rk, lg, lbias, nb, t)
            mix_s, rwkv_s_t = _rwkv_sample(_rwkv_prep(ps, ps_prev, params, ns, 1), rk, lg, lbias,
                                           state_rwkv_t, j, rwkv_s_t)
            rwkv_p.append(st_p)
            shift_p.append(_rmsnorm(xp.reshape(nb, t, d)[:, t - 1, :], g_attn, F32))
            shift_s.append(hs)
            w_out = b_w_out
        xo_p = _xattn_prompt(qp, q_blk, mem_k[layer].reshape(nb, n_mem, xdim),
                             mem_v[layer].reshape(nb, n_mem, xdim), nb, t)
        xo_s = _xattn_sample(qs, cache_mem_k, cache_mem_v, layer)
        xp = _matmul(mix_p, w_out, layer=j, a2=xo_p, res=xp, name="out_prompt")
        xs = _matmul(mix_s, w_out, layer=j, a2=xo_s, res=xs, name="out_sample")
        g_mlp = mlp_norm_g[layer]
        up = _matmul(_rmsnorm(xp, g_mlp, BF16), mlp_w1, layer=layer, act="relu2", out_dtype=BF16,
                     tn_cap=1024, name="mlp1_prompt")
        xp = _matmul(up, mlp_w2, layer=layer, res=xp, name="mlp2_prompt")
        us = _matmul(_rmsnorm(xs, g_mlp, F32), mlp_w1, layer=layer, act="relu2", name="mlp1_sample")
        xs = _matmul(us, mlp_w2, layer=layer, res=xs, name="mlp2_sample")

    y_p = _rmsnorm(xp, final_norm_g, F32).reshape(nb, t, d)
    y_s = _rmsnorm(xs, final_norm_g, F32).reshape(ns, 1, d)
    kv_shape = (depth, nb, n_mem, X_HEADS, X_HEAD_DIM)
    return (y_p, y_s, jnp.stack(mem_k).reshape(kv_shape), jnp.stack(mem_v).reshape(kv_shape),
            jnp.stack(hgrn_p), jnp.stack(rwkv_p), jnp.stack(shift_p),
            hgrn_s, jnp.transpose(rwkv_s_t, (0, 4, 1, 2, 3)), jnp.stack(shift_s))
```

```python
import functools

import jax
import jax.numpy as jnp
from jax import lax
from jax.experimental import pallas as pl
from jax.experimental.pallas import tpu as pltpu

F32 = jnp.float32
BF16 = jnp.bfloat16

RMS_EPS = 1e-6
GN_EPS = 64e-5
F_FLOOR = 1e-30
X_HEADS = 4
X_HEAD_DIM = 128
A_HEAD_DIM = 128
B_HEAD_DIM = 64
B_DECAY_LORA = 64
B_AAA_LORA = 64
B_GATE_LORA = 224

LANES = 128
SUBLANES = 8
LOG2E = 1.4426950408889634
VMEM_LIMIT = 56 * 1024 * 1024


def _cparams(*sem):
    return pltpu.CompilerParams(dimension_semantics=sem, vmem_limit_bytes=VMEM_LIMIT)


def _tile(n, cap, mult=8):
    if n <= cap:
        return n
    best = None
    for d in range(mult, cap + 1, mult):
        if n % d == 0:
            best = d
    assert best is not None, (n, cap, mult)
    return best


def _bdot(a, b):
    return jnp.dot(a.astype(BF16), b.astype(BF16), preferred_element_type=F32)


def _bdot_nt(a, b):
    return lax.dot_general(a.astype(BF16), b.astype(BF16), (((1,), (1,)), ((), ())),
                           preferred_element_type=F32)


def _bdot_tn(a, b):
    return lax.dot_general(a.astype(BF16), b.astype(BF16), (((0,), (0,)), ((), ())),
                           preferred_element_type=F32)


def _split3(x):
    hi = x.astype(BF16)
    r1 = x - hi.astype(F32)
    mid = r1.astype(BF16)
    lo = (r1 - mid.astype(F32)).astype(BF16)
    return hi, mid, lo


def _exact_ldot(m01, x):
    hi, mid, lo = _split3(x)
    d = lambda p: jnp.dot(m01, p, preferred_element_type=F32)
    return d(hi) + d(mid) + d(lo)


def _sigmoid(x):
    return 1.0 / (1.0 + jnp.exp(-x))


def _silu(x):
    return x * _sigmoid(x)


def _iota2(shape, axis):
    return lax.broadcasted_iota(jnp.int32, shape, axis)


def _rmsnorm_kernel(x_ref, g_ref, o_ref):
    x = x_ref[...]
    y = x * lax.rsqrt(jnp.mean(x * x, axis=-1, keepdims=True) + RMS_EPS)
    o_ref[...] = (y * g_ref[...]).astype(o_ref.dtype)


def _rmsnorm(x, g, out_dtype):
    m, d = x.shape
    tm = _tile(m, 512, 16)
    return pl.pallas_call(
        _rmsnorm_kernel,
        out_shape=jax.ShapeDtypeStruct((m, d), out_dtype),
        grid=(m // tm,),
        in_specs=[pl.BlockSpec((tm, d), lambda i: (i, 0)),
                  pl.BlockSpec((1, d), lambda i: (0, 0))],
        out_specs=pl.BlockSpec((tm, d), lambda i: (i, 0)),
        compiler_params=_cparams("parallel"),
        name="rmsnorm",
    )(x, g.reshape(1, d))


def _matmul_kernel(*refs, k1, nk, act, has_res, has_a2, wt):
    it = iter(refs)
    a1_ref = next(it)
    a2_ref = next(it) if has_a2 else None
    w_ref = next(it)
    res_ref = next(it) if has_res else None
    o_ref = next(it)

    if nk == 1:
        if wt:
            assert not has_a2
            acc = _bdot_nt(a1_ref[...], w_ref[...])
        else:
            acc = _bdot(a1_ref[...], w_ref[0:k1, :])
        if has_a2:
            acc = acc + _bdot(a2_ref[...], w_ref[k1:, :])
        if act == "relu2":
            acc = jnp.square(jnp.maximum(acc, 0.0))
        if has_res:
            acc = res_ref[...] + acc
        o_ref[...] = acc.astype(o_ref.dtype)
        return

    k = pl.program_id(2)
    part = _bdot(a1_ref[...], w_ref[...])

    @pl.when(k == 0)
    def _():
        o_ref[...] = (res_ref[...] + part) if has_res else part

    @pl.when(k > 0)
    def _():
        o_ref[...] += part


def _matmul(a1, w, *, layer=None, a2=None, res=None, act=None, out_dtype=F32, n_out=None, n_off=0,
            wt=False, tm_cap=2048, tn_cap=512, tk_cap=2048, name="matmul"):
    m, k1 = a1.shape
    k2 = a2.shape[1] if a2 is not None else 0
    kdim = k1 + k2
    n = w.shape[-2 if wt else -1] if n_out is None else n_out
    tm = _tile(m, tm_cap, 16 if a1.dtype == BF16 or out_dtype == BF16 else 8)
    tn = _tile(n, tn_cap, LANES)
    tk = _tile(kdim, tk_cap, LANES)
    nk = kdim // tk
    if nk > 1:
        assert a2 is None and act is None and out_dtype == F32 and not wt
    assert w.shape[-1 if wt else -2] == kdim and n_off % tn == 0
    j_off = n_off // tn
    w_block = (tn, tk) if wt else (tk, tn)
    w_idx = (lambda k, j: (j + j_off, k)) if wt else (lambda k, j: (k, j + j_off))
    if layer is None:
        w_spec = pl.BlockSpec(w_block, lambda i, j, k: w_idx(k, j))
    else:
        w_spec = pl.BlockSpec((None,) + w_block, lambda i, j, k: (layer,) + w_idx(k, j))
    in_specs = [pl.BlockSpec((tm, min(tk, k1)), lambda i, j, k: (i, k))]
    args = [a1]
    if a2 is not None:
        in_specs.append(pl.BlockSpec((tm, k2), lambda i, j, k: (i, 0)))
        args.append(a2)
    in_specs.append(w_spec)
    args.append(w)
    if res is not None:
        in_specs.append(pl.BlockSpec((tm, tn), lambda i, j, k: (i, j)))
        args.append(res)
    kern = functools.partial(_matmul_kernel, k1=k1, nk=nk, act=act,
                             has_res=res is not None, has_a2=a2 is not None, wt=wt)
    return pl.pallas_call(
        kern,
        out_shape=jax.ShapeDtypeStruct((m, n), out_dtype),
        grid=(m // tm, n // tn, nk),
        in_specs=in_specs,
        out_specs=pl.BlockSpec((tm, tn), lambda i, j, k: (i, j)),
        compiler_params=_cparams("parallel", "parallel", "arbitrary"),
        name=name,
    )(*args)


def _lower_bounds_kernel(lg_ref, o_ref):
    lg = lg_ref[...]
    e = jnp.exp(lg - jnp.max(lg, axis=0, keepdims=True))
    p = e / jnp.sum(e, axis=0, keepdims=True)
    run = jnp.zeros_like(p[0:1])
    for l in range(lg.shape[0]):
        run = run + p[l:l + 1]
        o_ref[l:l + 1, :] = run - p[0:1]


def _lower_bounds(logits):
    return pl.pallas_call(
        _lower_bounds_kernel,
        out_shape=jax.ShapeDtypeStruct(logits.shape, F32),
        name="hgrn_lower_bounds",
    )(logits)


def _softmax_rows(s):
    e = jnp.exp(s - jnp.max(s, axis=-1, keepdims=True))
    return e / jnp.sum(e, axis=-1, keepdims=True)


def _xattn_prompt_kernel(q_ref, k_ref, v_ref, o_ref):
    scale = X_HEAD_DIM ** -0.5
    for h in range(X_HEADS):
        sl = slice(h * X_HEAD_DIM, (h + 1) * X_HEAD_DIM)
        p = _softmax_rows(_bdot_nt(q_ref[:, sl], k_ref[:, sl]) * scale)
        o_ref[:, sl] = _bdot(p, v_ref[:, sl]).astype(o_ref.dtype)


def _xattn_prompt(q_arr, q_blk, mem_k, mem_v, nb, t):
    xdim = mem_k.shape[-1]
    n_mem = mem_k.shape[1]
    tq = _tile(t, 512, 16)
    nt = t // tq
    return pl.pallas_call(
        _xattn_prompt_kernel,
        out_shape=jax.ShapeDtypeStruct((nb * t, xdim), BF16),
        grid=(nb, nt),
        in_specs=[pl.BlockSpec((tq, xdim), lambda b, i: (b * nt + i, q_blk)),
                  pl.BlockSpec((None, n_mem, xdim), lambda b, i: (b, 0, 0)),
                  pl.BlockSpec((None, n_mem, xdim), lambda b, i: (b, 0, 0))],
        out_specs=pl.BlockSpec((tq, xdim), lambda b, i: (b * nt + i, 0)),
        compiler_params=_cparams("parallel", "parallel"),
        name="xattn_prompt",
    )(q_arr, mem_k, mem_v)


def _xattn_sample_kernel(q_ref, k_ref, v_ref, o_ref, *, bb):
    scale = X_HEAD_DIM ** -0.5
    nh = X_HEADS

    def fold(x):
        return x + pltpu.roll(x, nh, axis=0)

    def body(b, carry):
        q8 = q_ref[b] * scale
        s = jnp.sum(k_ref[b] * q8[None], axis=-1, keepdims=True)
        mx = jnp.broadcast_to(jnp.max(s, axis=0), (2 * nh, X_HEAD_DIM))
        mx = jnp.maximum(mx, pltpu.roll(mx, nh, axis=0))
        e = jnp.exp(s - mx[None])
        num = fold(jnp.sum(e * v_ref[b], axis=0))
        o_ref[b] = (num / fold(jnp.sum(e, axis=0)))[0:nh]
        return carry

    lax.fori_loop(0, bb, body, 0)


def _xattn_sample(q, cache_k, cache_v, layer):
    depth, nb, n_mem, nh, hd = cache_k.shape
    assert nh == X_HEADS and hd == X_HEAD_DIM and (n_mem * nh) % 8 == 0
    g = n_mem * nh // 8
    bb = 8
    q4 = q.reshape(nb, nh, hd)
    q8 = jnp.concatenate([q4, q4], axis=1)
    kv_spec = pl.BlockSpec((None, bb, g, 8, hd), lambda i: (layer, i, 0, 0, 0))
    out = pl.pallas_call(
        functools.partial(_xattn_sample_kernel, bb=bb),
        out_shape=jax.ShapeDtypeStruct((nb, nh, hd), F32),
        grid=(nb // bb,),
        in_specs=[pl.BlockSpec((bb, 2 * nh, hd), lambda i: (i, 0, 0)), kv_spec, kv_spec],
        out_specs=pl.BlockSpec((bb, nh, hd), lambda i: (i, 0, 0)),
        compiler_params=_cparams("parallel"),
        name="xattn_sample",
    )(q8, cache_k.reshape(depth, nb, g, 8, hd), cache_v.reshape(depth, nb, g, 8, hd))
    return out.reshape(nb, nh * hd)


HG_TB = 256
HG_C = 16


def _hgrn_prompt_kernel(q_ref, f_ref, i_ref, g_ref, lb_ref, og_ref, mix_ref, st_ref,
                        s_ref, qs_ref, kk_ref, bl_ref, o_ref, *, tb, c):
    t = pl.program_id(2)

    @pl.when(t == 0)
    def _():
        s_ref[...] = jnp.zeros_like(s_ref)

    lb = lb_ref[...]
    ff = f_ref[...]
    lg = jnp.log(jnp.maximum(lb + (1.0 - lb) * _sigmoid(ff), F_FLOOR)) * LOG2E
    kk_ref[...] = (1.0 - lb) * _sigmoid(-ff)
    qs_ref[...] = _silu(q_ref[...])
    shift = c.bit_length() - 1
    ri = _iota2((tb, tb), 0)
    ci = _iota2((tb, tb), 1)
    same_chunk = jnp.right_shift(ri, shift) == jnp.right_shift(ci, shift)
    lmat = jnp.where(same_chunk & (ci <= ri), 1.0, 0.0).astype(BF16)
    bl_ref[...] = _exact_ldot(lmat, lg)
    sub = SUBLANES
    rowi = _iota2((sub, 1), 0)

    nch = tb // c
    upd, dec = [], []
    for idx in range(nch):
        rows = slice(idx * c, (idx + 1) * c)
        bend = bl_ref[(idx + 1) * c - 1:(idx + 1) * c, :]
        upd.append(_bdot_tn(i_ref[rows, :], kk_ref[rows, :] * jnp.exp2(bend - bl_ref[rows, :])))
        dec.append(jnp.exp2(bend))
    states = [s_ref[...]]
    for idx in range(nch):
        states.append(states[idx] * dec[idx] + upd[idx])
    s_ref[...] = states[nch]
    for idx in range(nch):
        r0 = idx * c
        rows = slice(r0, r0 + c)
        o = _bdot_nt(qs_ref[rows, :] * jnp.exp2(bl_ref[rows, :]), states[idx])
        for h0 in range(0, c, sub):
            tile = slice(r0 + h0, r0 + h0 + sub)
            bt = bl_ref[tile, :]
            qt = qs_ref[tile, :]
            acc = o[h0:h0 + sub, :]
            for j in range(h0 + sub):
                r = r0 + j
                diff = bt - bl_ref[r:r + 1, :]
                if j >= h0:
                    a = jnp.sum(qt * kk_ref[r:r + 1, :] * jnp.exp2(jnp.minimum(diff, 0.0)),
                                axis=-1, keepdims=True)
                    a = jnp.where(rowi >= j - h0, a, 0.0)
                else:
                    a = jnp.sum(qt * kk_ref[r:r + 1, :] * jnp.exp2(diff), axis=-1, keepdims=True)
                acc = acc + a * i_ref[r:r + 1, :]
            o_ref[tile, :] = acc
    o = o_ref[...]
    y = o * lax.rsqrt(jnp.mean(o * o, axis=-1, keepdims=True) + RMS_EPS) * og_ref[...]
    mix_ref[...] = (y * _silu(g_ref[...])).astype(mix_ref.dtype)

    @pl.when(t == pl.num_programs(2) - 1)
    def _():
        st_ref[...] = s_ref[...].T


def _hgrn_prompt(p, lb, og, nb, t):
    mix = lb.shape[0]
    hd = A_HEAD_DIM
    nh = mix // hd
    tb = _tile(t, HG_TB, HG_C)
    nt = t // tb
    col = lambda off: pl.BlockSpec((tb, hd), lambda b, h, i: (b * nt + i, off + h))
    vec = pl.BlockSpec((1, hd), lambda b, h, i: (0, h))
    return pl.pallas_call(
        functools.partial(_hgrn_prompt_kernel, tb=tb, c=HG_C),
        out_shape=(jax.ShapeDtypeStruct((nb * t, mix), BF16),
                   jax.ShapeDtypeStruct((nb, nh, hd, hd), F32)),
        grid=(nb, nh, nt),
        in_specs=[col(0), col(nh), col(2 * nh), col(3 * nh), vec, vec],
        out_specs=(pl.BlockSpec((tb, hd), lambda b, h, i: (b * nt + i, h)),
                   pl.BlockSpec((None, None, hd, hd), lambda b, h, i: (b, h, 0, 0))),
        scratch_shapes=[pltpu.VMEM((hd, hd), F32)] + [pltpu.VMEM((tb, hd), F32)] * 4,
        compiler_params=_cparams("parallel", "parallel", "arbitrary"),
        name="hgrn_prompt",
    )(p, p, p, p, lb.reshape(1, mix), og.reshape(1, mix))


def _hgrn_sample_kernel(p_ref, lb_ref, og_ref, s_ref, *rest, bb):
    mix_ref, so_ref = rest[-2:]
    hd = s_ref.shape[-1]
    eye = (_iota2((hd, hd), 0) == _iota2((hd, hd), 1)).astype(F32)
    lb = lb_ref[...]
    col = lambda r: jnp.sum(eye[None] * r, axis=-1, keepdims=True)

    def body(b, carry):
        q, f, iv, g = p_ref[b, 0], p_ref[b, 1], p_ref[b, 2], p_ref[b, 3]
        dec = jnp.maximum(lb + (1.0 - lb) * _sigmoid(f), F_FLOOR)
        k = (1.0 - lb) * _sigmoid(-f)
        sn = s_ref[b] * col(dec) + col(k) * iv
        so_ref[b] = sn
        o = jnp.sum(sn * col(_silu(q)), axis=1, keepdims=True)
        y = o * lax.rsqrt(jnp.mean(o * o, axis=-1, keepdims=True) + RMS_EPS) * og_ref[...]
        mix_ref[b] = y * _silu(g)
        return carry

    lax.fori_loop(0, bb, body, 0)


def _stacked_state_io(states_shape, block, index_map, stacked):
    out_shape = jax.ShapeDtypeStruct(states_shape, F32)
    out_spec = pl.BlockSpec(block, index_map)
    if stacked is None:
        return out_shape, out_spec, [], []
    return out_shape, out_spec, [pl.BlockSpec(memory_space=pl.ANY)], [stacked]


def _hgrn_sample(p, lb, og, states, layer, stacked):
    mix = lb.shape[0]
    hd = A_HEAD_DIM
    nh = mix // hd
    nb = p.shape[0]
    bb = 4
    p5 = p[:, :4 * mix].reshape(nb, 4, nh, 1, hd)
    vec = pl.BlockSpec((nh, 1, hd), lambda i: (0, 0, 0))
    st_block = (None, bb, nh, hd, hd)
    st_map = lambda i: (layer, i, 0, 0, 0)
    so_shape, so_spec, alias_specs, alias_args = _stacked_state_io(states.shape, st_block, st_map, stacked)
    out, new_states = pl.pallas_call(
        functools.partial(_hgrn_sample_kernel, bb=bb),
        out_shape=(jax.ShapeDtypeStruct((nb, nh, 1, hd), F32), so_shape),
        grid=(nb // bb,),
        in_specs=[pl.BlockSpec((bb, 4, nh, 1, hd), lambda i: (i, 0, 0, 0, 0)), vec, vec,
                  pl.BlockSpec(st_block, st_map)] + alias_specs,
        out_specs=(pl.BlockSpec((bb, nh, 1, hd), lambda i: (i, 0, 0, 0)), so_spec),
        input_output_aliases={4: 1} if alias_args else {},
        compiler_params=_cparams("parallel"),
        name="hgrn_sample",
    )(p5, lb.reshape(nh, 1, hd), og.reshape(nh, 1, hd), states, *alias_args)
    return out.reshape(nb, mix), new_states


RW_TR = 128
RW_TB = 256
RW_C = 64
RW_PAIRS = 3


def _softplus(x):
    return jnp.maximum(x, 0.0) + jnp.log(1.0 + jnp.exp(-jnp.abs(x)))


def _seg_sum(x, seg):
    lo = _iota2(x.shape, x.ndim - 1) < seg
    s0 = jnp.sum(jnp.where(lo, x, 0.0), axis=-1, keepdims=True)
    s1 = jnp.sum(jnp.where(lo, 0.0, x), axis=-1, keepdims=True)
    return jnp.where(lo, s0, s1)


def _rwkv_prep_kernel(p_ref, prev_ref, mu_ref, w0_ref, a0_ref, kkw_ref, ka_ref, w2_ref, a2_ref, g2_ref,
                      r_o, k_o, v_o, kk_o, kka_o, w_o, g_o, *, mix):
    p = p_ref[...]
    xs = p + (prev_ref[...] - p) * mu_ref[...]
    r = xs[:, :mix]
    k = xs[:, mix:2 * mix]
    v = xs[:, 2 * mix:3 * mix]
    la = xs[:, 3 * mix:3 * mix + LANES]
    gl = xs[:, 3 * mix + LANES:3 * mix + 3 * LANES]
    w_raw = -_softplus(-(w0_ref[...] + _bdot(jnp.tanh(la), w2_ref[...]))) - 0.5
    a = _sigmoid(a0_ref[...] + _bdot(la, a2_ref[...]))
    r_o[...] = r
    v_o[...] = v
    w_o[...] = jnp.exp(-jnp.exp(w_raw))
    g_o[...] = _bdot(_sigmoid(gl), g2_ref[...])
    k_o[...] = k * (1.0 + (a - 1.0) * ka_ref[...])
    for j in range(mix // LANES):
        sl = slice(j * LANES, (j + 1) * LANES)
        kk = k[:, sl] * kkw_ref[:, sl]
        kk = kk / jnp.maximum(jnp.sqrt(_seg_sum(kk * kk, B_HEAD_DIM)), 1e-12)
        kk_o[:, sl] = kk
        kka_o[:, sl] = kk * a[:, sl]


def _rwkv_prep(p, prev, params):
    mu, w0, a0, kkw, ka, w2p, a2p, g2p = params
    mix = w0.shape[-1]
    rows, ncol = p.shape
    tr = _tile(rows, RW_TR, 8)
    row_in = pl.BlockSpec((tr, ncol), lambda i: (i, 0))
    row_out = pl.BlockSpec((tr, mix), lambda i: (i, 0))
    consts = [mu, w0, a0, kkw, ka, w2p, a2p, g2p]
    return pl.pallas_call(
        functools.partial(_rwkv_prep_kernel, mix=mix),
        out_shape=tuple(jax.ShapeDtypeStruct((rows, mix), F32) for _ in range(7)),
        grid=(rows // tr,),
        in_specs=[row_in, row_in] + [pl.BlockSpec(x.shape, lambda i: (0, 0)) for x in consts],
        out_specs=tuple(row_out for _ in range(7)),
        compiler_params=_cparams("parallel"),
        name="rwkv_prep",
    )(p, prev, *consts)


def _rwkv_scan_kernel(pr_ref, pk_ref, pv_ref, pl_ref, mur_ref, muk_ref, muv_ref, mul_ref,
                      w0_ref, a0_ref, kkw_ref, ka_ref, w2_ref, a2_ref, g2_ref,
                      rk_ref, lg_ref, lbias_ref, mix_ref, st_ref,
                      s_ref, cr_ref, ck_ref, cv_ref, cl_ref,
                      r_ref, k_ref, v_ref, kk_ref, kka_ref, lw_ref, g_ref, *, tb, c, pp):
    n2 = 2 * c
    assert n2 == LANES
    t = pl.program_id(2)

    @pl.when(t == 0)
    def _():
        for ref in (s_ref, cr_ref, ck_ref, cv_ref, cl_ref):
            ref[...] = jnp.zeros_like(ref)

    def shifted(p_ref, carry_ref, mu_ref):
        p = p_ref[...]
        prev = jnp.where(_iota2(p.shape, 0) == 0, carry_ref[...], pltpu.roll(p, 1, axis=0))
        carry_ref[...] = p[tb - 1:tb, :]
        return p + (prev - p) * mu_ref[...]

    xl = shifted(pl_ref, cl_ref, mul_ref)
    la = xl[:, 0:LANES]
    gl = xl[:, LANES:3 * LANES]
    w_raw = -_softplus(-(w0_ref[...] + _bdot(jnp.tanh(la), w2_ref[...]))) - 0.5
    lw_ref[...] = -jnp.exp(w_raw)
    a = _sigmoid(a0_ref[...] + _bdot(la, a2_ref[...]))
    g_ref[...] = _bdot(_sigmoid(gl), g2_ref[...])
    r_ref[...] = shifted(pr_ref, cr_ref, mur_ref)
    v_ref[...] = shifted(pv_ref, cv_ref, muv_ref)
    k_raw = shifted(pk_ref, ck_ref, muk_ref)
    k_ref[...] = k_raw * (1.0 + (a - 1.0) * ka_ref[...])
    for p in range(pp):
        sl = slice(p * LANES, (p + 1) * LANES)
        kk_p = k_raw[:, sl] * kkw_ref[:, sl]
        kk_p = kk_p / jnp.maximum(jnp.sqrt(_seg_sum(kk_p * kk_p, B_HEAD_DIM)), 1e-12)
        kk_ref[:, sl] = kk_p
        kka_ref[:, sl] = kk_p * a[:, sl]

    ri = _iota2((n2, n2), 0)
    ci = _iota2((n2, n2), 1)
    blk = lambda x, n: jnp.right_shift(x, n.bit_length() - 1)
    same_head = blk(ri, c) == blk(ci, c)
    strict = same_head & (ri > ci)
    incl = same_head & (ri >= ci)
    eye = (ri == ci).astype(F32)
    tril = (_iota2((c, c), 0) >= _iota2((c, c), 1)).astype(BF16)
    head0 = _iota2((c, LANES), 1) < B_HEAD_DIM
    hd = float(B_HEAD_DIM)

    def sibling(n):
        return (blk(ri, 2 * n) == blk(ci, 2 * n)) & (blk(ri, n) > blk(ci, n))

    def stack(x):
        return jnp.concatenate([jnp.where(head0, x, 0.0), jnp.where(head0, 0.0, x)], axis=0)

    nch = tb // c
    inst = [(p, ch) for p in range(pp) for ch in range(nch)]
    each = lambda f, *lists: [f(*xs) for xs in zip(*lists)]
    win = lambda ref: [ref[ch * c:(ch + 1) * c, p * LANES:(p + 1) * LANES] for p, ch in inst]
    r, k, v, kk, kka, lw = (win(x) for x in (r_ref, k_ref, v_ref, kk_ref, kka_ref, lw_ref))
    cs = each(lambda x: _exact_ldot(tril, x), lw)
    gend = each(lambda x: jnp.exp(x[c - 1:c, :]), cs)
    ginv = each(lambda x: jnp.exp(-x), cs)
    ph = each(jnp.multiply, kka, ginv)
    kh = each(jnp.multiply, k, ginv)
    kk_m = each(lambda a, b, d: stack(a * jnp.exp(b - d)), kk, cs, lw)
    r_m = each(lambda a, b: stack(a * jnp.exp(b)), r, cs)
    v_m = each(stack, v)
    lhs = each(lambda a, b: jnp.concatenate([a, b], axis=0), kk_m, r_m)
    gp = each(lambda a, b: _bdot_nt(a, stack(b)), lhs, ph)
    gk = each(lambda a, b: _bdot_nt(a, stack(b)), lhs, kh)
    a_mat = each(lambda g: jnp.where(strict, g[0:n2], 0.0), gp)
    b_mat = each(lambda g: jnp.where(strict, g[0:n2], 0.0), gk)
    rp = each(lambda g: jnp.where(incl, g[n2:], 0.0), gp)
    rk = each(lambda g: jnp.where(incl, g[n2:], 0.0), gk)
    bv = each(_bdot, b_mat, v_m)
    rkv = each(_bdot, rk, v_m)
    t_m = each(lambda a: eye - jnp.where(blk(ri, 2) == blk(ci, 2), a, 0.0), a_mat)
    n = 2
    while n < c:
        y = each(lambda a, tm: _bdot(jnp.where(sibling(n), a, 0.0), tm), a_mat, t_m)
        t_m = each(lambda tm, yy: tm - _bdot(tm, yy), t_m, y)
        n *= 2
    ku = each(lambda tm, a, b: _bdot(tm, jnp.concatenate([a, b], axis=1)), t_m, kk_m, bv)
    rpku = each(_bdot, rp, ku)
    ml = each(lambda a, b, g: _bdot_tn(a, stack(b * g)), ku, ph, gend)
    vk = each(lambda a, b, g: _bdot_tn(a, stack(b * g)), v_m, kh, gend)
    r_t = each(lambda a, b: a - b[:, 0:n2], r_m, rpku)
    o_0 = each(lambda a, b: a - b[:, n2:], rkv, rpku)
    cst = each(lambda a, b: a - b[n2:], vk, ml)
    s = [s_ref[p] for p in range(pp)]
    o2 = [None] * len(inst)
    for ch in range(nch):
        for p in range(pp):
            i = p * nch + ch
            o2[i] = _bdot_nt(r_t[i], s[p]) + o_0[i]
            s[p] = s[p] * gend[i] - _bdot(s[p], ml[i][0:n2]) + cst[i]
    for p in range(pp):
        s_ref[p] = s[p]
    for i, (p, ch) in enumerate(inst):
        rows = slice(ch * c, (ch + 1) * c)
        cols = slice(p * LANES, (p + 1) * LANES)
        o = o2[i][0:c, :] + o2[i][c:n2, :]
        d = o - _seg_sum(o, B_HEAD_DIM) / hd
        on = (d * lax.rsqrt(_seg_sum(d * d, B_HEAD_DIM) / hd + GN_EPS) * lg_ref[:, cols]
              + lbias_ref[:, cols])
        bonus = _seg_sum(r[i] * k[i] * rk_ref[:, cols], B_HEAD_DIM) * v[i]
        mix_ref[rows, cols] = ((on + bonus) * g_ref[rows, cols]).astype(mix_ref.dtype)

    @pl.when(t == pl.num_programs(2) - 1)
    def _():
        for p in range(pp):
            s = s_ref[p]
            st_ref[2 * p] = s[0:c, 0:c]
            st_ref[2 * p + 1] = s[c:n2, c:n2]


def _rwkv_scan(p_main, p_lora, params, rk, lg, lbias, nb, t):
    mu, w0, a0, kkw, ka, w2p, a2p, g2p = params
    mix = rk.shape[-1]
    hd = B_HEAD_DIM
    nh = mix // hd
    pp = RW_PAIRS
    wc = pp * LANES
    nblk = mix // wc
    nl = p_lora.shape[1]
    assert nh % (2 * pp) == 0 and p_main.shape[1] == 3 * mix and mu.shape[1] >= 3 * mix + nl
    tb = _tile(t, RW_TB, RW_C)
    nt = t // tb
    col = lambda off: pl.BlockSpec((tb, wc), lambda b, h, i: (b * nt + i, off + h))
    vec = lambda off=0: pl.BlockSpec((1, wc), lambda b, h, i: (0, off + h))
    mat = lambda x: pl.BlockSpec((x.shape[0], wc), lambda b, h, i: (0, h))
    mu_l = lax.slice(mu, (0, 3 * mix), (1, 3 * mix + nl))
    vec1 = lambda x: x.reshape(1, mix)
    return pl.pallas_call(
        functools.partial(_rwkv_scan_kernel, tb=tb, c=RW_C, pp=pp),
        out_shape=(jax.ShapeDtypeStruct((nb * t, mix), BF16),
                   jax.ShapeDtypeStruct((nb, nh, hd, hd), F32)),
        grid=(nb, nblk, nt),
        in_specs=[col(0), col(nblk), col(2 * nblk),
                  pl.BlockSpec((tb, nl), lambda b, h, i: (b * nt + i, 0)),
                  vec(0), vec(nblk), vec(2 * nblk), pl.BlockSpec((1, nl), lambda b, h, i: (0, 0)),
                  vec(), vec(), vec(), vec(), mat(w2p), mat(a2p), mat(g2p), vec(), vec(), vec()],
        out_specs=(col(0), pl.BlockSpec((None, 2 * pp, hd, hd), lambda b, h, i: (b, h, 0, 0))),
        scratch_shapes=([pltpu.VMEM((pp, LANES, LANES), F32)] + [pltpu.VMEM((1, wc), F32)] * 3
                        + [pltpu.VMEM((1, nl), F32)] + [pltpu.VMEM((tb, wc), F32)] * 7),
        compiler_params=_cparams("parallel", "parallel", "arbitrary"),
        name="rwkv_scan",
    )(p_main, p_main, p_main, p_lora, mu, mu, mu, mu_l, w0, a0, kkw, ka, w2p, a2p, g2p,
      vec1(rk), vec1(lg), vec1(lbias))


def _rwkv_sample_kernel(r_ref, k_ref, v_ref, kk_ref, kka_ref, w_ref, g_ref, rk_ref, lg_ref, lbias_ref,
                        s_ref, *rest):
    o_ref, so_ref, vt_ref, ot_ref = rest[-4:]
    hd = s_ref.shape[1]
    r = r_ref[...]
    k = k_ref[...]
    r_t, k_t, w_t, kka_t = r.T, k.T, w_ref[...].T, kka_ref[...].T
    nkk_t = -(kk_ref[...].T)
    vt_ref[...] = v_ref[...].T
    bon_t = (r * k * rk_ref[...]).T
    for h in range(2):
        sl = slice(h * hd, (h + 1) * hd)
        nkk, kka, w, kf, rr = nkk_t[sl], kka_t[sl], w_t[sl], k_t[sl], r_t[sl]

        def body(i, carry, h=h, nkk=nkk, kka=kka, w=w, kf=kf, rr=rr):
            row = pl.ds(h * hd + i, 1)
            sv = s_ref[h, i]
            sa = jnp.sum(sv * nkk, axis=0, keepdims=True)
            sn = sv * w + sa * kka + vt_ref[row, :] * kf
            so_ref[h, i] = sn
            ot_ref[row, :] = jnp.sum(sn * rr, axis=0, keepdims=True)
            return carry

        lax.fori_loop(0, hd, body, 0, unroll=4)
    dn, bonus = [], []
    for h in range(2):
        sl = slice(h * hd, (h + 1) * hd)
        o = ot_ref[sl, :]
        d = o - jnp.mean(o, axis=0, keepdims=True)
        dn.append(d * lax.rsqrt(jnp.mean(d * d, axis=0, keepdims=True) + GN_EPS))
        bonus.append(jnp.sum(bon_t[sl], axis=0, keepdims=True) * vt_ref[sl, :])
    dn = jnp.concatenate(dn, axis=0).T
    bonus = jnp.concatenate(bonus, axis=0).T
    o_ref[...] = (dn * lg_ref[...] + lbias_ref[...] + bonus) * g_ref[...]


def _rwkv_sample(prep, rk, lg, lbias, states_t, layer, stacked):
    mix = rk.shape[-1]
    hd = B_HEAD_DIM
    nb = prep[0].shape[0]
    assert nb == LANES
    col = pl.BlockSpec((nb, LANES), lambda h: (0, h))
    vec = pl.BlockSpec((1, LANES), lambda h: (0, h))
    st_block = (None, 2, hd, hd, nb)
    st_map = lambda h: (layer, h, 0, 0, 0)
    so_shape, so_spec, alias_specs, alias_args = _stacked_state_io(states_t.shape, st_block, st_map, stacked)
    return pl.pallas_call(
        _rwkv_sample_kernel,
        out_shape=(jax.ShapeDtypeStruct((nb, mix), F32), so_shape),
        grid=(mix // LANES,),
        in_specs=[col] * 7 + [vec] * 3 + [pl.BlockSpec(st_block, st_map)] + alias_specs,
        out_specs=(col, so_spec),
        scratch_shapes=[pltpu.VMEM((LANES, nb), F32)] * 2,
        input_output_aliases={11: 1} if alias_args else {},
        compiler_params=_cparams("parallel"),
        name="rwkv_sample",
    )(*prep, rk.reshape(1, mix), lg.reshape(1, mix), lbias.reshape(1, mix), states_t, *alias_args)


def kernel(x_prompt, x_sample, cache_mem_k, cache_mem_v, state_hgrn, state_rwkv, state_rwkv_shift,
           mem_prompt, attn_norm_g, mlp_norm_g, final_norm_g, mem_norm_g, wk_mem, wv_mem,
           a_w_in, a_w_out, a_lb_logits, a_onorm_g,
           b_w_in, b_w_out, b_mu, b_w0, b_w2, b_a0, b_a2, b_g2, b_k_k, b_k_a, b_r_k, b_lnx_g, b_lnx_b,
           mlp_w1, mlp_w2):
    nb, t, d = x_prompt.shape
    ns = x_sample.shape[0]
    depth = attn_norm_g.shape[0]
    mix = a_onorm_g.shape[-1]
    xdim = d - mix
    n_mem = mem_prompt.shape[1]
    b_cols = b_mu.shape[-1]
    b_pad = -(-b_cols // LANES) * LANES
    lora0 = 3 * mix

    lbs = _lower_bounds(a_lb_logits)

    m = _rmsnorm(mem_prompt.reshape(nb * n_mem, d), mem_norm_g, BF16)
    mem_k = [_matmul(m, wk_mem, layer=l, name="mem_k") for l in range(depth)]
    mem_v = [_matmul(m, wv_mem, layer=l, name="mem_v") for l in range(depth)]
    state_rwkv_t = jnp.transpose(state_rwkv, (0, 2, 3, 4, 1))
    b_w_in_t = jnp.transpose(b_w_in, (0, 2, 1))

    xp = x_prompt.reshape(nb * t, d)
    xs = x_sample.reshape(ns, d)
    hgrn_p, rwkv_p, shift_p, shift_s = [], [], [], []
    hgrn_s = rwkv_s_t = None
    for layer in range(depth):
        j = layer // 2
        g_attn = attn_norm_g[layer]
        hp = _rmsnorm(xp, g_attn, BF16)
        hs = _rmsnorm(xs, g_attn, F32)
        if layer % 2 == 0:
            pp = _matmul(hp, a_w_in, layer=j, name="a_in_prompt")
            ps = _matmul(hs, a_w_in, layer=j, name="a_in_sample")
            mix_p, st_p = _hgrn_prompt(pp, lbs[j], a_onorm_g[j], nb, t)
            mix_s, hgrn_s = _hgrn_sample(ps, lbs[j], a_onorm_g[j], state_hgrn, j, hgrn_s)
            hgrn_p.append(st_p)
            qp, q_blk = pp, (4 * mix) // xdim
            qs = ps[:, 4 * mix:]
            w_out = a_w_out
        else:
            zeros = lambda n: jnp.zeros((n, mix), F32)
            params = (
                jnp.pad(b_mu[j], (0, b_pad - b_cols)).reshape(1, b_pad),
                b_w0[j].reshape(1, mix), b_a0[j].reshape(1, mix),
                b_k_k[j].reshape(1, mix), b_k_a[j].reshape(1, mix),
                jnp.concatenate([b_w2[j], zeros(LANES - B_DECAY_LORA)], axis=0),
                jnp.concatenate([zeros(B_DECAY_LORA), b_a2[j]], axis=0),
                jnp.concatenate([b_g2[j], zeros(2 * LANES - B_GATE_LORA)], axis=0),
            )
            assert lora0 + B_DECAY_LORA + B_AAA_LORA + B_GATE_LORA == b_cols
            w_q = lax.slice(b_w_in_t, (j, b_cols, 0), (j + 1, b_w_in_t.shape[1], d)).reshape(xdim, d)
            pp = _matmul(hp, b_w_in_t, layer=j, n_out=lora0, wt=True, name="b_in_prompt")
            pp_lora = _matmul(hp, b_w_in_t, layer=j, n_out=b_pad - lora0, n_off=lora0, wt=True,
                              name="b_lora_prompt")
            ps = _matmul(hs, b_w_in_t, layer=j, n_out=b_pad, wt=True, name="b_in_sample")
            ps_prev = _matmul(state_rwkv_shift[j], b_w_in_t, layer=j, n_out=b_pad, wt=True,
                              name="b_in_shift")
            qp = _matmul(hp, w_q, wt=True, name="b_q_prompt")
            qs = _matmul(hs, w_q, wt=True, name="b_q_sample")
            q_blk = 0
            rk, lg, lbias = b_r_k[j].reshape(mix), b_lnx_g[j], b_lnx_b[j]
            mix_p, st_p = _rwkv_scan(pp, pp_lora, params, rk, lg, lbias, nb, t)
            mix_s, rwkv_s_t = _rwkv_sample(_rwkv_prep(ps, ps_prev, params), rk, lg, lbias,
                                           state_rwkv_t, j, rwkv_s_t)
            rwkv_p.append(st_p)
            shift_p.append(_rmsnorm(xp.reshape(nb, t, d)[:, t - 1, :], g_attn, F32))
            shift_s.append(hs)
            w_out = b_w_out
        xo_p = _xattn_prompt(qp, q_blk, mem_k[layer].reshape(nb, n_mem, xdim),
                             mem_v[layer].reshape(nb, n_mem, xdim), nb, t)
        xo_s = _xattn_sample(qs, cache_mem_k, cache_mem_v, layer)
        xp = _matmul(mix_p, w_out, layer=j, a2=xo_p, res=xp, name="out_prompt")
        xs = _matmul(mix_s, w_out, layer=j, a2=xo_s, res=xs, name="out_sample")
        g_mlp = mlp_norm_g[layer]
        up = _matmul(_rmsnorm(xp, g_mlp, BF16), mlp_w1, layer=layer, act="relu2", out_dtype=BF16,
                     tn_cap=1024, name="mlp1_prompt")
        xp = _matmul(up, mlp_w2, layer=layer, res=xp, tm_cap=1024, tn_cap=256, tk_cap=up.shape[1],
                     name="mlp2_prompt")
        us = _matmul(_rmsnorm(xs, g_mlp, F32), mlp_w1, layer=layer, act="relu2", name="mlp1_sample")
        xs = _matmul(us, mlp_w2, layer=layer, res=xs, name="mlp2_sample")

    y_p = _rmsnorm(xp, final_norm_g, F32).reshape(nb, t, d)
    y_s = _rmsnorm(xs, final_norm_g, F32).reshape(ns, 1, d)
    kv_shape = (depth, nb, n_mem, X_HEADS, X_HEAD_DIM)
    return (y_p, y_s, jnp.stack(mem_k).reshape(kv_shape), jnp.stack(mem_v).reshape(kv_shape),
            jnp.stack(hgrn_p), jnp.stack(rwkv_p), jnp.stack(shift_p),
            hgrn_s, jnp.transpose(rwkv_s_t, (0, 4, 1, 2, 3)), jnp.stack(shift_s))
```

```python
import functools

import jax
import jax.numpy as jnp
from jax import lax
from jax.experimental import pallas as pl
from jax.experimental.pallas import tpu as pltpu

F32 = jnp.float32
BF16 = jnp.bfloat16

RMS_EPS = 1e-6
GN_EPS = 64e-5
F_FLOOR = 1e-30
X_HEADS = 4
X_HEAD_DIM = 128
A_HEAD_DIM = 128
B_HEAD_DIM = 64
B_DECAY_LORA = 64
B_AAA_LORA = 64
B_GATE_LORA = 224

LANES = 128
SUBLANES = 8
LOG2E = 1.4426950408889634
VMEM_LIMIT = 60 * 1024 * 1024
MM_TM = 2080
MLP2_TM = 1040


def _cparams(*sem):
    return pltpu.CompilerParams(dimension_semantics=sem, vmem_limit_bytes=VMEM_LIMIT)


def _tile(n, cap, mult=8):
    if n <= cap:
        return n
    best = None
    for d in range(mult, cap + 1, mult):
        if n % d == 0:
            best = d
    assert best is not None, (n, cap, mult)
    return best


def _bdot(a, b):
    return jnp.dot(a.astype(BF16), b.astype(BF16), preferred_element_type=F32)


def _bdot_nt(a, b):
    return lax.dot_general(a.astype(BF16), b.astype(BF16), (((1,), (1,)), ((), ())),
                           preferred_element_type=F32)


def _bdot_tn(a, b):
    return lax.dot_general(a.astype(BF16), b.astype(BF16), (((0,), (0,)), ((), ())),
                           preferred_element_type=F32)


def _split3(x):
    hi = x.astype(BF16)
    r1 = x - hi.astype(F32)
    mid = r1.astype(BF16)
    lo = (r1 - mid.astype(F32)).astype(BF16)
    return hi, mid, lo


def _exact_ldot(m01, x):
    hi, mid, lo = _split3(x)
    d = lambda p: jnp.dot(m01, p, preferred_element_type=F32)
    return d(hi) + d(mid) + d(lo)


def _sigmoid(x):
    return 1.0 / (1.0 + jnp.exp(-x))


def _silu(x):
    return x * _sigmoid(x)


def _iota2(shape, axis):
    return lax.broadcasted_iota(jnp.int32, shape, axis)


def _rmsnorm_kernel(x_ref, g_ref, o_ref):
    x = x_ref[...]
    y = x * lax.rsqrt(jnp.mean(x * x, axis=-1, keepdims=True) + RMS_EPS)
    o_ref[...] = (y * g_ref[...]).astype(o_ref.dtype)


def _rmsnorm(x, g, out_dtype, row0=0, nrows=None):
    d = x.shape[1]
    m = x.shape[0] - row0 if nrows is None else nrows
    tm = _tile(m, 512, 16)
    assert row0 % tm == 0
    off = row0 // tm
    return pl.pallas_call(
        _rmsnorm_kernel,
        out_shape=jax.ShapeDtypeStruct((m, d), out_dtype),
        grid=(m // tm,),
        in_specs=[pl.BlockSpec((tm, d), lambda i: (i + off, 0)),
                  pl.BlockSpec((1, d), lambda i: (0, 0))],
        out_specs=pl.BlockSpec((tm, d), lambda i: (i, 0)),
        compiler_params=_cparams("parallel"),
        name="rmsnorm",
    )(x, g.reshape(1, d))


def _matmul_kernel(*refs, k1, nk, act, has_res, has_a2, wt):
    it = iter(refs)
    a1_ref = next(it)
    a2_ref = next(it) if has_a2 else None
    w_ref = next(it)
    res_ref = next(it) if has_res else None
    o_ref = next(it)

    if nk == 1:
        if wt:
            assert not has_a2
            acc = _bdot_nt(a1_ref[...], w_ref[...])
        else:
            acc = _bdot(a1_ref[...], w_ref[0:k1, :])
        if has_a2:
            acc = acc + _bdot(a2_ref[...], w_ref[k1:, :])
        if act == "relu2":
            acc = jnp.square(jnp.maximum(acc, 0.0))
        if has_res:
            acc = res_ref[...] + acc
        o_ref[...] = acc.astype(o_ref.dtype)
        return

    k = pl.program_id(2)
    part = _bdot(a1_ref[...], w_ref[...])

    @pl.when(k == 0)
    def _():
        o_ref[...] = (res_ref[...] + part) if has_res else part

    @pl.when(k > 0)
    def _():
        o_ref[...] += part


def _matmul(a1, w, *, layer=None, a2=None, res=None, act=None, out_dtype=F32, n_out=None, n_off=0,
            wt=False, tm_cap=MM_TM, tn_cap=512, tk_cap=2048, name="matmul"):
    m, k1 = a1.shape
    k2 = a2.shape[1] if a2 is not None else 0
    kdim = k1 + k2
    n = w.shape[-2 if wt else -1] if n_out is None else n_out
    tm = _tile(m, tm_cap, 16 if a1.dtype == BF16 or out_dtype == BF16 else 8)
    tn = _tile(n, tn_cap, LANES)
    tk = _tile(kdim, tk_cap, LANES)
    nk = kdim // tk
    if nk > 1:
        assert a2 is None and act is None and out_dtype == F32 and not wt
    assert w.shape[-1 if wt else -2] == kdim and n_off % tn == 0
    j_off = n_off // tn
    w_block = (tn, tk) if wt else (tk, tn)
    w_idx = (lambda k, j: (j + j_off, k)) if wt else (lambda k, j: (k, j + j_off))
    if layer is None:
        w_spec = pl.BlockSpec(w_block, lambda i, j, k: w_idx(k, j))
    else:
        w_spec = pl.BlockSpec((None,) + w_block, lambda i, j, k: (layer,) + w_idx(k, j))
    in_specs = [pl.BlockSpec((tm, min(tk, k1)), lambda i, j, k: (i, k))]
    args = [a1]
    if a2 is not None:
        in_specs.append(pl.BlockSpec((tm, k2), lambda i, j, k: (i, 0)))
        args.append(a2)
    in_specs.append(w_spec)
    args.append(w)
    if res is not None:
        in_specs.append(pl.BlockSpec((tm, tn), lambda i, j, k: (i, j)))
        args.append(res)
    kern = functools.partial(_matmul_kernel, k1=k1, nk=nk, act=act,
                             has_res=res is not None, has_a2=a2 is not None, wt=wt)
    return pl.pallas_call(
        kern,
        out_shape=jax.ShapeDtypeStruct((m, n), out_dtype),
        grid=(m // tm, n // tn, nk),
        in_specs=in_specs,
        out_specs=pl.BlockSpec((tm, tn), lambda i, j, k: (i, j)),
        compiler_params=_cparams("parallel", "parallel", "arbitrary"),
        name=name,
    )(*args)


def _lower_bounds_kernel(lg_ref, o_ref):
    lg = lg_ref[...]
    e = jnp.exp(lg - jnp.max(lg, axis=0, keepdims=True))
    p = e / jnp.sum(e, axis=0, keepdims=True)
    run = jnp.zeros_like(p[0:1])
    for l in range(lg.shape[0]):
        run = run + p[l:l + 1]
        o_ref[l:l + 1, :] = run - p[0:1]


def _lower_bounds(logits):
    return pl.pallas_call(
        _lower_bounds_kernel,
        out_shape=jax.ShapeDtypeStruct(logits.shape, F32),
        name="hgrn_lower_bounds",
    )(logits)


def _softmax_rows(s):
    e = jnp.exp(s - jnp.max(s, axis=-1, keepdims=True))
    return e / jnp.sum(e, axis=-1, keepdims=True)


def _xattn_prompt_kernel(q_ref, k_ref, v_ref, o_ref):
    scale = X_HEAD_DIM ** -0.5
    for h in range(X_HEADS):
        sl = slice(h * X_HEAD_DIM, (h + 1) * X_HEAD_DIM)
        p = _softmax_rows(_bdot_nt(q_ref[:, sl], k_ref[:, sl]) * scale)
        o_ref[:, sl] = _bdot(p, v_ref[:, sl]).astype(o_ref.dtype)


def _xattn_prompt(q_arr, q_blk, mem_k, mem_v, nb, t):
    xdim = mem_k.shape[-1]
    n_mem = mem_k.shape[1]
    tq = _tile(t, 512, 16)
    nt = t // tq
    return pl.pallas_call(
        _xattn_prompt_kernel,
        out_shape=jax.ShapeDtypeStruct((nb * t, xdim), BF16),
        grid=(nb, nt),
        in_specs=[pl.BlockSpec((tq, xdim), lambda b, i: (b * nt + i, q_blk)),
                  pl.BlockSpec((None, n_mem, xdim), lambda b, i: (b, 0, 0)),
                  pl.BlockSpec((None, n_mem, xdim), lambda b, i: (b, 0, 0))],
        out_specs=pl.BlockSpec((tq, xdim), lambda b, i: (b * nt + i, 0)),
        compiler_params=_cparams("parallel", "parallel"),
        name="xattn_prompt",
    )(q_arr, mem_k, mem_v)


def _xattn_sample_kernel(q_ref, k_ref, v_ref, o_ref, *, bb):
    scale = X_HEAD_DIM ** -0.5
    nh = X_HEADS

    def fold(x):
        return x + pltpu.roll(x, nh, axis=0)

    def body(b, carry):
        q8 = q_ref[b] * scale
        s = jnp.sum(k_ref[b] * q8[None], axis=-1, keepdims=True)
        mx = jnp.broadcast_to(jnp.max(s, axis=0), (2 * nh, X_HEAD_DIM))
        mx = jnp.maximum(mx, pltpu.roll(mx, nh, axis=0))
        e = jnp.exp(s - mx[None])
        num = fold(jnp.sum(e * v_ref[b], axis=0))
        o_ref[b] = (num / fold(jnp.sum(e, axis=0)))[0:nh]
        return carry

    lax.fori_loop(0, bb, body, 0)


def _xattn_sample(q, cache_k, cache_v, layer):
    depth, nb, n_mem, nh, hd = cache_k.shape
    assert nh == X_HEADS and hd == X_HEAD_DIM and (n_mem * nh) % 8 == 0
    g = n_mem * nh // 8
    bb = 8
    q4 = q.reshape(nb, nh, hd)
    q8 = jnp.concatenate([q4, q4], axis=1)
    kv_spec = pl.BlockSpec((None, bb, g, 8, hd), lambda i: (layer, i, 0, 0, 0))
    out = pl.pallas_call(
        functools.partial(_xattn_sample_kernel, bb=bb),
        out_shape=jax.ShapeDtypeStruct((nb, nh, hd), F32),
        grid=(nb // bb,),
        in_specs=[pl.BlockSpec((bb, 2 * nh, hd), lambda i: (i, 0, 0)), kv_spec, kv_spec],
        out_specs=pl.BlockSpec((bb, nh, hd), lambda i: (i, 0, 0)),
        compiler_params=_cparams("parallel"),
        name="xattn_sample",
    )(q8, cache_k.reshape(depth, nb, g, 8, hd), cache_v.reshape(depth, nb, g, 8, hd))
    return out.reshape(nb, nh * hd)


HG_TB = 256
HG_C = 16


def _hgrn_prompt_kernel(q_ref, f_ref, i_ref, g_ref, lb_ref, og_ref, mix_ref, st_ref,
                        s_ref, qs_ref, kk_ref, bl_ref, o_ref, *, tb, c):
    t = pl.program_id(2)

    @pl.when(t == 0)
    def _():
        s_ref[...] = jnp.zeros_like(s_ref)

    lb = lb_ref[...]
    ff = f_ref[...]
    lg = jnp.log(jnp.maximum(lb + (1.0 - lb) * _sigmoid(ff), F_FLOOR)) * LOG2E
    kk_ref[...] = (1.0 - lb) * _sigmoid(-ff)
    qs_ref[...] = _silu(q_ref[...])
    shift = c.bit_length() - 1
    ri = _iota2((tb, tb), 0)
    ci = _iota2((tb, tb), 1)
    same_chunk = jnp.right_shift(ri, shift) == jnp.right_shift(ci, shift)
    lmat = jnp.where(same_chunk & (ci <= ri), 1.0, 0.0).astype(BF16)
    bl_ref[...] = _exact_ldot(lmat, lg)
    sub = SUBLANES
    rowi = _iota2((sub, 1), 0)

    nch = tb // c
    upd, dec = [], []
    for idx in range(nch):
        rows = slice(idx * c, (idx + 1) * c)
        bend = bl_ref[(idx + 1) * c - 1:(idx + 1) * c, :]
        upd.append(_bdot_tn(i_ref[rows, :], kk_ref[rows, :] * jnp.exp2(bend - bl_ref[rows, :])))
        dec.append(jnp.exp2(bend))
    states = [s_ref[...]]
    for idx in range(nch):
        states.append(states[idx] * dec[idx] + upd[idx])
    s_ref[...] = states[nch]
    for idx in range(nch):
        r0 = idx * c
        rows = slice(r0, r0 + c)
        o = _bdot_nt(qs_ref[rows, :] * jnp.exp2(bl_ref[rows, :]), states[idx])
        for h0 in range(0, c, sub):
            tile = slice(r0 + h0, r0 + h0 + sub)
            bt = bl_ref[tile, :]
            qt = qs_ref[tile, :]
            acc = o[h0:h0 + sub, :]
            for j in range(h0 + sub):
                r = r0 + j
                diff = bt - bl_ref[r:r + 1, :]
                if j >= h0:
                    a = jnp.sum(qt * kk_ref[r:r + 1, :] * jnp.exp2(jnp.minimum(diff, 0.0)),
                                axis=-1, keepdims=True)
                    a = jnp.where(rowi >= j - h0, a, 0.0)
                else:
                    a = jnp.sum(qt * kk_ref[r:r + 1, :] * jnp.exp2(diff), axis=-1, keepdims=True)
                acc = acc + a * i_ref[r:r + 1, :]
            o_ref[tile, :] = acc
    o = o_ref[...]
    y = o * lax.rsqrt(jnp.mean(o * o, axis=-1, keepdims=True) + RMS_EPS) * og_ref[...]
    mix_ref[...] = (y * _silu(g_ref[...])).astype(mix_ref.dtype)

    @pl.when(t == pl.num_programs(2) - 1)
    def _():
        st_ref[...] = s_ref[...].T


def _hgrn_prompt(p, lb, og, nb, t):
    mix = lb.shape[0]
    hd = A_HEAD_DIM
    nh = mix // hd
    tb = _tile(t, HG_TB, HG_C)
    nt = t // tb
    col = lambda off: pl.BlockSpec((tb, hd), lambda b, h, i: (b * nt + i, off + h))
    vec = pl.BlockSpec((1, hd), lambda b, h, i: (0, h))
    return pl.pallas_call(
        functools.partial(_hgrn_prompt_kernel, tb=tb, c=HG_C),
        out_shape=(jax.ShapeDtypeStruct((nb * t, mix), BF16),
                   jax.ShapeDtypeStruct((nb, nh, hd, hd), F32)),
        grid=(nb, nh, nt),
        in_specs=[col(0), col(nh), col(2 * nh), col(3 * nh), vec, vec],
        out_specs=(pl.BlockSpec((tb, hd), lambda b, h, i: (b * nt + i, h)),
                   pl.BlockSpec((None, None, hd, hd), lambda b, h, i: (b, h, 0, 0))),
        scratch_shapes=[pltpu.VMEM((hd, hd), F32)] + [pltpu.VMEM((tb, hd), F32)] * 4,
        compiler_params=_cparams("parallel", "parallel", "arbitrary"),
        name="hgrn_prompt",
    )(p, p, p, p, lb.reshape(1, mix), og.reshape(1, mix))


def _layer_view(so_ref, layer, first):
    if not first:
        return so_ref
    for l in range(so_ref.shape[0]):
        if l != layer:
            so_ref[l] = jnp.zeros(so_ref.shape[1:], so_ref.dtype)
    return so_ref.at[layer]


def _hgrn_sample_kernel(p_ref, lb_ref, og_ref, s_ref, *rest, bb, layer, first):
    mix_ref, so_ref = rest[-2:]
    so_ref = _layer_view(so_ref, layer, first)
    hd = s_ref.shape[-1]
    eye = (_iota2((hd, hd), 0) == _iota2((hd, hd), 1)).astype(F32)
    lb = lb_ref[...]
    col = lambda r: jnp.sum(eye[None] * r, axis=-1, keepdims=True)

    def body(b, carry):
        q, f, iv, g = p_ref[b, 0], p_ref[b, 1], p_ref[b, 2], p_ref[b, 3]
        dec = jnp.maximum(lb + (1.0 - lb) * _sigmoid(f), F_FLOOR)
        k = (1.0 - lb) * _sigmoid(-f)
        sn = s_ref[b] * col(dec) + col(k) * iv
        so_ref[b] = sn
        o = jnp.sum(sn * col(_silu(q)), axis=1, keepdims=True)
        y = o * lax.rsqrt(jnp.mean(o * o, axis=-1, keepdims=True) + RMS_EPS) * og_ref[...]
        mix_ref[b] = y * _silu(g)
        return carry

    lax.fori_loop(0, bb, body, 0)


def _stacked_state_io(states_shape, block, index_map, stacked):
    out_shape = jax.ShapeDtypeStruct(states_shape, F32)
    if stacked is None:
        all_layers = pl.BlockSpec((states_shape[0],) + tuple(block[1:]),
                                  lambda *idx: (0,) + tuple(index_map(*idx)[1:]))
        return out_shape, all_layers, [], []
    return out_shape, pl.BlockSpec(block, index_map), [pl.BlockSpec(memory_space=pl.ANY)], [stacked]


def _hgrn_sample(p, lb, og, states, layer, stacked):
    mix = lb.shape[0]
    hd = A_HEAD_DIM
    nh = mix // hd
    nb = p.shape[0]
    bb = 4
    p5 = p[:, :4 * mix].reshape(nb, 4, nh, 1, hd)
    vec = pl.BlockSpec((nh, 1, hd), lambda i: (0, 0, 0))
    st_block = (None, bb, nh, hd, hd)
    st_map = lambda i: (layer, i, 0, 0, 0)
    so_shape, so_spec, alias_specs, alias_args = _stacked_state_io(states.shape, st_block, st_map, stacked)
    out, new_states = pl.pallas_call(
        functools.partial(_hgrn_sample_kernel, bb=bb, layer=layer, first=stacked is None),
        out_shape=(jax.ShapeDtypeStruct((nb, nh, 1, hd), F32), so_shape),
        grid=(nb // bb,),
        in_specs=[pl.BlockSpec((bb, 4, nh, 1, hd), lambda i: (i, 0, 0, 0, 0)), vec, vec,
                  pl.BlockSpec(st_block, st_map)] + alias_specs,
        out_specs=(pl.BlockSpec((bb, nh, 1, hd), lambda i: (i, 0, 0, 0)), so_spec),
        input_output_aliases={4: 1} if alias_args else {},
        compiler_params=_cparams("parallel"),
        name="hgrn_sample",
    )(p5, lb.reshape(nh, 1, hd), og.reshape(nh, 1, hd), states, *alias_args)
    return out.reshape(nb, mix), new_states


RW_TR = 128
RW_TB = 256
RW_C = 64
RW_PAIRS = 3


def _softplus(x):
    return jnp.maximum(x, 0.0) + jnp.log(1.0 + jnp.exp(-jnp.abs(x)))


def _seg_sum(x, seg):
    lo = _iota2(x.shape, x.ndim - 1) < seg
    s0 = jnp.sum(jnp.where(lo, x, 0.0), axis=-1, keepdims=True)
    s1 = jnp.sum(jnp.where(lo, 0.0, x), axis=-1, keepdims=True)
    return jnp.where(lo, s0, s1)


def _rwkv_prep_kernel(p_ref, prev_ref, mu_ref, w0_ref, a0_ref, kkw_ref, ka_ref, w2_ref, a2_ref, g2_ref,
                      r_o, k_o, v_o, kk_o, kka_o, w_o, g_o, *, mix):
    p = p_ref[...]
    xs = p + (prev_ref[...] - p) * mu_ref[...]
    r = xs[:, :mix]
    k = xs[:, mix:2 * mix]
    v = xs[:, 2 * mix:3 * mix]
    la = xs[:, 3 * mix:3 * mix + LANES]
    gl = xs[:, 3 * mix + LANES:3 * mix + 3 * LANES]
    w_raw = -_softplus(-(w0_ref[...] + _bdot(jnp.tanh(la), w2_ref[...]))) - 0.5
    a = _sigmoid(a0_ref[...] + _bdot(la, a2_ref[...]))
    r_o[...] = r
    v_o[...] = v
    w_o[...] = jnp.exp(-jnp.exp(w_raw))
    g_o[...] = _bdot(_sigmoid(gl), g2_ref[...])
    k_o[...] = k * (1.0 + (a - 1.0) * ka_ref[...])
    for j in range(mix // LANES):
        sl = slice(j * LANES, (j + 1) * LANES)
        kk = k[:, sl] * kkw_ref[:, sl]
        kk = kk / jnp.maximum(jnp.sqrt(_seg_sum(kk * kk, B_HEAD_DIM)), 1e-12)
        kk_o[:, sl] = kk
        kka_o[:, sl] = kk * a[:, sl]


def _rwkv_prep(p, prev, params):
    mu, w0, a0, kkw, ka, w2p, a2p, g2p = params
    mix = w0.shape[-1]
    rows, ncol = p.shape
    tr = _tile(rows, RW_TR, 8)
    row_in = pl.BlockSpec((tr, ncol), lambda i: (i, 0))
    row_out = pl.BlockSpec((tr, mix), lambda i: (i, 0))
    consts = [mu, w0, a0, kkw, ka, w2p, a2p, g2p]
    return pl.pallas_call(
        functools.partial(_rwkv_prep_kernel, mix=mix),
        out_shape=tuple(jax.ShapeDtypeStruct((rows, mix), F32) for _ in range(7)),
        grid=(rows // tr,),
        in_specs=[row_in, row_in] + [pl.BlockSpec(x.shape, lambda i: (0, 0)) for x in consts],
        out_specs=tuple(row_out for _ in range(7)),
        compiler_params=_cparams("parallel"),
        name="rwkv_prep",
    )(p, prev, *consts)


def _rwkv_scan_kernel(pr_ref, pk_ref, pv_ref, pl_ref, mur_ref, muk_ref, muv_ref, mul_ref,
                      w0_ref, a0_ref, kkw_ref, ka_ref, w2_ref, a2_ref, g2_ref,
                      rk_ref, lg_ref, lbias_ref, mix_ref, st_ref,
                      s_ref, cr_ref, ck_ref, cv_ref, cl_ref,
                      r_ref, k_ref, v_ref, kk_ref, kka_ref, lw_ref, g_ref, *, tb, c, pp):
    n2 = 2 * c
    assert n2 == LANES
    t = pl.program_id(2)

    @pl.when(t == 0)
    def _():
        for ref in (s_ref, cr_ref, ck_ref, cv_ref, cl_ref):
            ref[...] = jnp.zeros_like(ref)

    def shifted(p_ref, carry_ref, mu_ref):
        p = p_ref[...]
        prev = jnp.where(_iota2(p.shape, 0) == 0, carry_ref[...], pltpu.roll(p, 1, axis=0))
        carry_ref[...] = p[tb - 1:tb, :]
        return p + (prev - p) * mu_ref[...]

    xl = shifted(pl_ref, cl_ref, mul_ref)
    la = xl[:, 0:LANES]
    gl = xl[:, LANES:3 * LANES]
    w_raw = -_softplus(-(w0_ref[...] + _bdot(jnp.tanh(la), w2_ref[...]))) - 0.5
    lw_ref[...] = -jnp.exp(w_raw)
    a = _sigmoid(a0_ref[...] + _bdot(la, a2_ref[...]))
    g_ref[...] = _bdot(_sigmoid(gl), g2_ref[...])
    r_ref[...] = shifted(pr_ref, cr_ref, mur_ref)
    v_ref[...] = shifted(pv_ref, cv_ref, muv_ref)
    k_raw = shifted(pk_ref, ck_ref, muk_ref)
    k_ref[...] = k_raw * (1.0 + (a - 1.0) * ka_ref[...])
    for p in range(pp):
        sl = slice(p * LANES, (p + 1) * LANES)
        kk_p = k_raw[:, sl] * kkw_ref[:, sl]
        kk_p = kk_p / jnp.maximum(jnp.sqrt(_seg_sum(kk_p * kk_p, B_HEAD_DIM)), 1e-12)
        kk_ref[:, sl] = kk_p
        kka_ref[:, sl] = kk_p * a[:, sl]

    ri = _iota2((n2, n2), 0)
    ci = _iota2((n2, n2), 1)
    blk = lambda x, n: jnp.right_shift(x, n.bit_length() - 1)
    same_head = blk(ri, c) == blk(ci, c)
    strict = same_head & (ri > ci)
    incl = same_head & (ri >= ci)
    eye = (ri == ci).astype(F32)
    tril = (_iota2((c, c), 0) >= _iota2((c, c), 1)).astype(BF16)
    head0 = _iota2((c, LANES), 1) < B_HEAD_DIM
    hd = float(B_HEAD_DIM)

    def sibling(n):
        return (blk(ri, 2 * n) == blk(ci, 2 * n)) & (blk(ri, n) > blk(ci, n))

    def stack(x):
        return jnp.concatenate([jnp.where(head0, x, 0.0), jnp.where(head0, 0.0, x)], axis=0)

    nch = tb // c
    inst = [(p, ch) for p in range(pp) for ch in range(nch)]
    each = lambda f, *lists: [f(*xs) for xs in zip(*lists)]
    win = lambda ref: [ref[ch * c:(ch + 1) * c, p * LANES:(p + 1) * LANES] for p, ch in inst]
    r, k, v, kk, kka, lw = (win(x) for x in (r_ref, k_ref, v_ref, kk_ref, kka_ref, lw_ref))
    cs = each(lambda x: _exact_ldot(tril, x), lw)
    gend = each(lambda x: jnp.exp(x[c - 1:c, :]), cs)
    ginv = each(lambda x: jnp.exp(-x), cs)
    ph = each(jnp.multiply, kka, ginv)
    kh = each(jnp.multiply, k, ginv)
    kk_m = each(lambda a, b, d: stack(a * jnp.exp(b - d)), kk, cs, lw)
    r_m = each(lambda a, b: stack(a * jnp.exp(b)), r, cs)
    v_m = each(stack, v)
    lhs = each(lambda a, b: jnp.concatenate([a, b], axis=0), kk_m, r_m)
    gp = each(lambda a, b: _bdot_nt(a, stack(b)), lhs, ph)
    gk = each(lambda a, b: _bdot_nt(a, stack(b)), lhs, kh)
    a_mat = each(lambda g: jnp.where(strict, g[0:n2], 0.0), gp)
    b_mat = each(lambda g: jnp.where(strict, g[0:n2], 0.0), gk)
    rp = each(lambda g: jnp.where(incl, g[n2:], 0.0), gp)
    rk = each(lambda g: jnp.where(incl, g[n2:], 0.0), gk)
    bv = each(_bdot, b_mat, v_m)
    rkv = each(_bdot, rk, v_m)
    t_m = each(lambda a: eye - jnp.where(blk(ri, 2) == blk(ci, 2), a, 0.0), a_mat)
    n = 2
    while n < c:
        y = each(lambda a, tm: _bdot(jnp.where(sibling(n), a, 0.0), tm), a_mat, t_m)
        t_m = each(lambda tm, yy: tm - _bdot(tm, yy), t_m, y)
        n *= 2
    ku = each(lambda tm, a, b: _bdot(tm, jnp.concatenate([a, b], axis=1)), t_m, kk_m, bv)
    rpku = each(_bdot, rp, ku)
    ml = each(lambda a, b, g: _bdot_tn(a, stack(b * g)), ku, ph, gend)
    vk = each(lambda a, b, g: _bdot_tn(a, stack(b * g)), v_m, kh, gend)
    r_t = each(lambda a, b: a - b[:, 0:n2], r_m, rpku)
    o_0 = each(lambda a, b: a - b[:, n2:], rkv, rpku)
    cst = each(lambda a, b: a - b[n2:], vk, ml)
    s = [s_ref[p] for p in range(pp)]
    o2 = [None] * len(inst)
    for ch in range(nch):
        for p in range(pp):
            i = p * nch + ch
            o2[i] = _bdot_nt(r_t[i], s[p]) + o_0[i]
            s[p] = s[p] * gend[i] - _bdot(s[p], ml[i][0:n2]) + cst[i]
    for p in range(pp):
        s_ref[p] = s[p]
    for i, (p, ch) in enumerate(inst):
        rows = slice(ch * c, (ch + 1) * c)
        cols = slice(p * LANES, (p + 1) * LANES)
        o = o2[i][0:c, :] + o2[i][c:n2, :]
        d = o - _seg_sum(o, B_HEAD_DIM) / hd
        on = (d * lax.rsqrt(_seg_sum(d * d, B_HEAD_DIM) / hd + GN_EPS) * lg_ref[:, cols]
              + lbias_ref[:, cols])
        bonus = _seg_sum(r[i] * k[i] * rk_ref[:, cols], B_HEAD_DIM) * v[i]
        mix_ref[rows, cols] = ((on + bonus) * g_ref[rows, cols]).astype(mix_ref.dtype)

    @pl.when(t == pl.num_programs(2) - 1)
    def _():
        for p in range(pp):
            s = s_ref[p]
            st_ref[2 * p] = s[0:c, 0:c]
            st_ref[2 * p + 1] = s[c:n2, c:n2]


def _rwkv_scan(p_main, p_lora, params, rk, lg, lbias, nb, t):
    mu, w0, a0, kkw, ka, w2p, a2p, g2p = params
    mix = rk.shape[-1]
    hd = B_HEAD_DIM
    nh = mix // hd
    pp = RW_PAIRS
    wc = pp * LANES
    nblk = mix // wc
    nl = p_lora.shape[1]
    assert nh % (2 * pp) == 0 and p_main.shape[1] == 3 * mix and mu.shape[1] >= 3 * mix + nl
    tb = _tile(t, RW_TB, RW_C)
    nt = t // tb
    col = lambda off: pl.BlockSpec((tb, wc), lambda b, h, i: (b * nt + i, off + h))
    vec = lambda off=0: pl.BlockSpec((1, wc), lambda b, h, i: (0, off + h))
    mat = lambda x: pl.BlockSpec((x.shape[0], wc), lambda b, h, i: (0, h))
    mu_l = lax.slice(mu, (0, 3 * mix), (1, 3 * mix + nl))
    vec1 = lambda x: x.reshape(1, mix)
    return pl.pallas_call(
        functools.partial(_rwkv_scan_kernel, tb=tb, c=RW_C, pp=pp),
        out_shape=(jax.ShapeDtypeStruct((nb * t, mix), BF16),
                   jax.ShapeDtypeStruct((nb, nh, hd, hd), F32)),
        grid=(nb, nblk, nt),
        in_specs=[col(0), col(nblk), col(2 * nblk),
                  pl.BlockSpec((tb, nl), lambda b, h, i: (b * nt + i, 0)),
                  vec(0), vec(nblk), vec(2 * nblk), pl.BlockSpec((1, nl), lambda b, h, i: (0, 0)),
                  vec(), vec(), vec(), vec(), mat(w2p), mat(a2p), mat(g2p), vec(), vec(), vec()],
        out_specs=(col(0), pl.BlockSpec((None, 2 * pp, hd, hd), lambda b, h, i: (b, h, 0, 0))),
        scratch_shapes=([pltpu.VMEM((pp, LANES, LANES), F32)] + [pltpu.VMEM((1, wc), F32)] * 3
                        + [pltpu.VMEM((1, nl), F32)] + [pltpu.VMEM((tb, wc), F32)] * 7),
        compiler_params=_cparams("parallel", "parallel", "arbitrary"),
        name="rwkv_scan",
    )(p_main, p_main, p_main, p_lora, mu, mu, mu, mu_l, w0, a0, kkw, ka, w2p, a2p, g2p,
      vec1(rk), vec1(lg), vec1(lbias))


def _rwkv_sample_kernel(r_ref, k_ref, v_ref, kk_ref, kka_ref, w_ref, g_ref, rk_ref, lg_ref, lbias_ref,
                        s_ref, *rest, layer, first):
    o_ref, so_ref, vt_ref, ot_ref = rest[-4:]
    so_ref = _layer_view(so_ref, layer, first)
    hd = s_ref.shape[1]
    r = r_ref[...]
    k = k_ref[...]
    r_t, k_t, w_t, kka_t = r.T, k.T, w_ref[...].T, kka_ref[...].T
    nkk_t = -(kk_ref[...].T)
    vt_ref[...] = v_ref[...].T
    bon_t = (r * k * rk_ref[...]).T
    for h in range(2):
        sl = slice(h * hd, (h + 1) * hd)
        nkk, kka, w, kf, rr = nkk_t[sl], kka_t[sl], w_t[sl], k_t[sl], r_t[sl]

        def body(i, carry, h=h, nkk=nkk, kka=kka, w=w, kf=kf, rr=rr):
            row = pl.ds(h * hd + i, 1)
            sv = s_ref[h, i]
            sa = jnp.sum(sv * nkk, axis=0, keepdims=True)
            sn = sv * w + sa * kka + vt_ref[row, :] * kf
            so_ref[h, i] = sn
            ot_ref[row, :] = jnp.sum(sn * rr, axis=0, keepdims=True)
            return carry

        lax.fori_loop(0, hd, body, 0, unroll=4)
    dn, bonus = [], []
    for h in range(2):
        sl = slice(h * hd, (h + 1) * hd)
        o = ot_ref[sl, :]
        d = o - jnp.mean(o, axis=0, keepdims=True)
        dn.append(d * lax.rsqrt(jnp.mean(d * d, axis=0, keepdims=True) + GN_EPS))
        bonus.append(jnp.sum(bon_t[sl], axis=0, keepdims=True) * vt_ref[sl, :])
    dn = jnp.concatenate(dn, axis=0).T
    bonus = jnp.concatenate(bonus, axis=0).T
    o_ref[...] = (dn * lg_ref[...] + lbias_ref[...] + bonus) * g_ref[...]


def _rwkv_sample(prep, rk, lg, lbias, states_t, layer, stacked):
    mix = rk.shape[-1]
    hd = B_HEAD_DIM
    nb = prep[0].shape[0]
    assert nb == LANES
    col = pl.BlockSpec((nb, LANES), lambda h: (0, h))
    vec = pl.BlockSpec((1, LANES), lambda h: (0, h))
    st_block = (None, 2, hd, hd, nb)
    st_map = lambda h: (layer, h, 0, 0, 0)
    so_shape, so_spec, alias_specs, alias_args = _stacked_state_io(states_t.shape, st_block, st_map, stacked)
    return pl.pallas_call(
        functools.partial(_rwkv_sample_kernel, layer=layer, first=stacked is None),
        out_shape=(jax.ShapeDtypeStruct((nb, mix), F32), so_shape),
        grid=(mix // LANES,),
        in_specs=[col] * 7 + [vec] * 3 + [pl.BlockSpec(st_block, st_map)] + alias_specs,
        out_specs=(col, so_spec),
        scratch_shapes=[pltpu.VMEM((LANES, nb), F32)] * 2,
        input_output_aliases={11: 1} if alias_args else {},
        compiler_params=_cparams("parallel"),
        name="rwkv_sample",
    )(*prep, rk.reshape(1, mix), lg.reshape(1, mix), lbias.reshape(1, mix), states_t, *alias_args)


def kernel(x_prompt, x_sample, cache_mem_k, cache_mem_v, state_hgrn, state_rwkv, state_rwkv_shift,
           mem_prompt, attn_norm_g, mlp_norm_g, final_norm_g, mem_norm_g, wk_mem, wv_mem,
           a_w_in, a_w_out, a_lb_logits, a_onorm_g,
           b_w_in, b_w_out, b_mu, b_w0, b_w2, b_a0, b_a2, b_g2, b_k_k, b_k_a, b_r_k, b_lnx_g, b_lnx_b,
           mlp_w1, mlp_w2):
    nb, t, d = x_prompt.shape
    ns = x_sample.shape[0]
    depth = attn_norm_g.shape[0]
    mix = a_onorm_g.shape[-1]
    xdim = d - mix
    n_mem = mem_prompt.shape[1]
    b_cols = b_mu.shape[-1]
    b_pad = -(-b_cols // LANES) * LANES
    lora0 = 3 * mix

    lbs = _lower_bounds(a_lb_logits)

    m = _rmsnorm(mem_prompt.reshape(nb * n_mem, d), mem_norm_g, BF16)
    mem_k = [_matmul(m, wk_mem, layer=l, name="mem_k") for l in range(depth)]
    mem_v = [_matmul(m, wv_mem, layer=l, name="mem_v") for l in range(depth)]
    state_rwkv_t = jnp.transpose(state_rwkv, (0, 2, 3, 4, 1))
    b_w_in_t = jnp.transpose(b_w_in, (0, 2, 1))

    rp = nb * t
    x = jnp.concatenate([x_prompt.reshape(rp, d), x_sample.reshape(ns, d)], axis=0)
    rows_of = lambda a: a[rp:rp + ns]
    join = lambda a_p, a_s: jnp.concatenate([a_p, a_s.astype(a_p.dtype)], axis=0)
    hgrn_p, rwkv_p, shift_p, shift_s = [], [], [], []
    hgrn_s = rwkv_s_t = None
    for layer in range(depth):
        j = layer // 2
        g_attn = attn_norm_g[layer]
        h = _rmsnorm(x, g_attn, BF16)
        if layer % 2 == 0:
            p = _matmul(h, a_w_in, layer=j, name="a_in")
            mix_p, st_p = _hgrn_prompt(p, lbs[j], a_onorm_g[j], nb, t)
            mix_s, hgrn_s = _hgrn_sample(rows_of(p), lbs[j], a_onorm_g[j], state_hgrn, j, hgrn_s)
            hgrn_p.append(st_p)
            q_arr, q_blk = p, (4 * mix) // xdim
            qs = rows_of(p)[:, 4 * mix:]
            w_out = a_w_out
        else:
            zeros = lambda n: jnp.zeros((n, mix), F32)
            params = (
                jnp.pad(b_mu[j], (0, b_pad - b_cols)).reshape(1, b_pad),
                b_w0[j].reshape(1, mix), b_a0[j].reshape(1, mix),
                b_k_k[j].reshape(1, mix), b_k_a[j].reshape(1, mix),
                jnp.concatenate([b_w2[j], zeros(LANES - B_DECAY_LORA)], axis=0),
                jnp.concatenate([zeros(B_DECAY_LORA), b_a2[j]], axis=0),
                jnp.concatenate([b_g2[j], zeros(2 * LANES - B_GATE_LORA)], axis=0),
            )
            assert lora0 + B_DECAY_LORA + B_AAA_LORA + B_GATE_LORA == b_cols
            w_q = lax.slice(b_w_in_t, (j, b_cols, 0), (j + 1, b_w_in_t.shape[1], d)).reshape(xdim, d)
            p = _matmul(h, b_w_in_t, layer=j, n_out=lora0, wt=True, name="b_in")
            p_lora = _matmul(h, b_w_in_t, layer=j, n_out=b_pad - lora0, n_off=lora0, wt=True,
                             name="b_lora")
            q_arr, q_blk = _matmul(h, w_q, wt=True, name="b_q"), 0
            qs = rows_of(q_arr)
            ps = jnp.concatenate([rows_of(p), rows_of(p_lora)], axis=1)
            ps_prev = _matmul(state_rwkv_shift[j], b_w_in_t, layer=j, n_out=b_pad, wt=True,
                              name="b_in_shift")
            rk, lg, lbias = b_r_k[j].reshape(mix), b_lnx_g[j], b_lnx_b[j]
            mix_p, st_p = _rwkv_scan(p, p_lora, params, rk, lg, lbias, nb, t)
            mix_s, rwkv_s_t = _rwkv_sample(_rwkv_prep(ps, ps_prev, params), rk, lg, lbias,
                                           state_rwkv_t, j, rwkv_s_t)
            rwkv_p.append(st_p)
            shift_p.append(_rmsnorm(x[t - 1:rp:t], g_attn, F32))
            shift_s.append(_rmsnorm(x, g_attn, F32, rp, ns))
            w_out = b_w_out
        xo_p = _xattn_prompt(q_arr, q_blk, mem_k[layer].reshape(nb, n_mem, xdim),
                             mem_v[layer].reshape(nb, n_mem, xdim), nb, t)
        xo_s = _xattn_sample(qs, cache_mem_k, cache_mem_v, layer)
        x = _matmul(join(mix_p, mix_s), w_out, layer=j, a2=join(xo_p, xo_s), res=x, name="out_proj")
        u = _matmul(_rmsnorm(x, mlp_norm_g[layer], BF16), mlp_w1, layer=layer, act="relu2",
                    out_dtype=BF16, tn_cap=1024, name="mlp1")
        x = _matmul(u, mlp_w2, layer=layer, res=x, tm_cap=MLP2_TM, tn_cap=256, tk_cap=u.shape[1],
                    name="mlp2")

    y_p = _rmsnorm(x, final_norm_g, F32, 0, rp).reshape(nb, t, d)
    y_s = _rmsnorm(x, final_norm_g, F32, rp, ns).reshape(ns, 1, d)
    kv_shape = (depth, nb, n_mem, X_HEADS, X_HEAD_DIM)
    return (y_p, y_s, jnp.stack(mem_k).reshape(kv_shape), jnp.stack(mem_v).reshape(kv_shape),
            jnp.stack(hgrn_p), jnp.stack(rwkv_p), jnp.stack(shift_p),
            hgrn_s, jnp.transpose(rwkv_s_t, (0, 4, 1, 2, 3)), jnp.stack(shift_s))
```

```python
import functools

import jax
import jax.numpy as jnp
from jax import lax
from jax.experimental import pallas as pl
from jax.experimental.pallas import tpu as pltpu

F32 = jnp.float32
BF16 = jnp.bfloat16

RMS_EPS = 1e-6
GN_EPS = 64e-5
F_FLOOR = 1e-30
X_HEADS = 4
X_HEAD_DIM = 128
A_HEAD_DIM = 128
B_HEAD_DIM = 64
B_DECAY_LORA = 64
B_AAA_LORA = 64
B_GATE_LORA = 224

LANES = 128
SUBLANES = 8
LOG2E = 1.4426950408889634
VMEM_LIMIT = 60 * 1024 * 1024
MM_TM = 2080
MLP2_TM = 1040


def _cparams(*sem):
    return pltpu.CompilerParams(dimension_semantics=sem, vmem_limit_bytes=VMEM_LIMIT)


def _tile(n, cap, mult=8):
    if n <= cap:
        return n
    best = None
    for d in range(mult, cap + 1, mult):
        if n % d == 0:
            best = d
    assert best is not None, (n, cap, mult)
    return best


def _bdot(a, b):
    return jnp.dot(a.astype(BF16), b.astype(BF16), preferred_element_type=F32)


def _bdot_nt(a, b):
    return lax.dot_general(a.astype(BF16), b.astype(BF16), (((1,), (1,)), ((), ())),
                           preferred_element_type=F32)


def _bdot_tn(a, b):
    return lax.dot_general(a.astype(BF16), b.astype(BF16), (((0,), (0,)), ((), ())),
                           preferred_element_type=F32)


def _split3(x):
    hi = x.astype(BF16)
    r1 = x - hi.astype(F32)
    mid = r1.astype(BF16)
    lo = (r1 - mid.astype(F32)).astype(BF16)
    return hi, mid, lo


def _exact_ldot(m01, x):
    hi, mid, lo = _split3(x)
    d = lambda p: jnp.dot(m01, p, preferred_element_type=F32)
    return d(hi) + d(mid) + d(lo)


def _sigmoid(x):
    return 1.0 / (1.0 + jnp.exp(-x))


def _silu(x):
    return x * _sigmoid(x)


def _iota2(shape, axis):
    return lax.broadcasted_iota(jnp.int32, shape, axis)


def _rmsnorm_kernel(x_ref, g_ref, o_ref):
    x = x_ref[...]
    y = x * lax.rsqrt(jnp.mean(x * x, axis=-1, keepdims=True) + RMS_EPS)
    o_ref[...] = (y * g_ref[...]).astype(o_ref.dtype)


def _rmsnorm(x, g, out_dtype, row0=0, nrows=None):
    d = x.shape[1]
    m = x.shape[0] - row0 if nrows is None else nrows
    tm = _tile(m, 512, 16)
    assert row0 % tm == 0
    off = row0 // tm
    return pl.pallas_call(
        _rmsnorm_kernel,
        out_shape=jax.ShapeDtypeStruct((m, d), out_dtype),
        grid=(m // tm,),
        in_specs=[pl.BlockSpec((tm, d), lambda i: (i + off, 0)),
                  pl.BlockSpec((1, d), lambda i: (0, 0))],
        out_specs=pl.BlockSpec((tm, d), lambda i: (i, 0)),
        compiler_params=_cparams("parallel"),
        name="rmsnorm",
    )(x, g.reshape(1, d))


def _matmul_kernel(*refs, k1, nk, act, has_res, has_a2, wt):
    it = iter(refs)
    a1_ref = next(it)
    a2_ref = next(it) if has_a2 else None
    w_ref = next(it)
    res_ref = next(it) if has_res else None
    o_ref = next(it)

    if nk == 1:
        if wt:
            assert not has_a2
            acc = _bdot_nt(a1_ref[...], w_ref[...])
        else:
            acc = _bdot(a1_ref[...], w_ref[0:k1, :])
        if has_a2:
            acc = acc + _bdot(a2_ref[...], w_ref[k1:, :])
        if act == "relu2":
            acc = jnp.square(jnp.maximum(acc, 0.0))
        if has_res:
            acc = res_ref[...] + acc
        o_ref[...] = acc.astype(o_ref.dtype)
        return

    k = pl.program_id(2)
    part = _bdot(a1_ref[...], w_ref[...])

    @pl.when(k == 0)
    def _():
        o_ref[...] = (res_ref[...] + part) if has_res else part

    @pl.when(k > 0)
    def _():
        o_ref[...] += part


def _matmul(a1, w, *, layer=None, a2=None, res=None, res_row0=None, act=None, out_dtype=F32,
            n_out=None, n_off=0, wt=False, tm_cap=MM_TM, tn_cap=512, tk_cap=2048, name="matmul"):
    m, k1 = a1.shape
    k2 = a2.shape[1] if a2 is not None else 0
    kdim = k1 + k2
    n = w.shape[-2 if wt else -1] if n_out is None else n_out
    tm = _tile(m, tm_cap, 16 if a1.dtype == BF16 or out_dtype == BF16 else 8)
    tn = _tile(n, tn_cap, LANES)
    tk = _tile(kdim, tk_cap, LANES)
    nk = kdim // tk
    if nk > 1:
        assert a2 is None and act is None and out_dtype == F32 and not wt
    assert w.shape[-1 if wt else -2] == kdim and n_off % tn == 0
    j_off = n_off // tn
    w_block = (tn, tk) if wt else (tk, tn)
    w_idx = (lambda k, j: (j + j_off, k)) if wt else (lambda k, j: (k, j + j_off))
    if layer is None:
        w_spec = pl.BlockSpec(w_block, lambda i, j, k: w_idx(k, j))
    else:
        w_spec = pl.BlockSpec((None,) + w_block, lambda i, j, k: (layer,) + w_idx(k, j))
    in_specs = [pl.BlockSpec((tm, min(tk, k1)), lambda i, j, k: (i, k))]
    args = [a1]
    if a2 is not None:
        in_specs.append(pl.BlockSpec((tm, k2), lambda i, j, k: (i, 0)))
        args.append(a2)
    in_specs.append(w_spec)
    args.append(w)
    out_shape = jax.ShapeDtypeStruct((m, n), out_dtype)
    out_spec = pl.BlockSpec((tm, tn), lambda i, j, k: (i, j))
    aliases = {}
    if res is not None:
        if res_row0 is not None:
            assert res_row0 % tm == 0 and res.shape[1] == n and res.dtype == out_dtype
            i_off = res_row0 // tm
            out_spec = pl.BlockSpec((tm, tn), lambda i, j, k: (i + i_off, j))
            out_shape = jax.ShapeDtypeStruct(res.shape, out_dtype)
            aliases = {len(args): 0}
        in_specs.append(out_spec)
        args.append(res)
    kern = functools.partial(_matmul_kernel, k1=k1, nk=nk, act=act,
                             has_res=res is not None, has_a2=a2 is not None, wt=wt)
    return pl.pallas_call(
        kern,
        out_shape=out_shape,
        grid=(m // tm, n // tn, nk),
        in_specs=in_specs,
        out_specs=out_spec,
        input_output_aliases=aliases,
        compiler_params=_cparams("parallel", "parallel", "arbitrary"),
        name=name,
    )(*args)


def _lower_bounds_kernel(lg_ref, o_ref):
    lg = lg_ref[...]
    e = jnp.exp(lg - jnp.max(lg, axis=0, keepdims=True))
    p = e / jnp.sum(e, axis=0, keepdims=True)
    run = jnp.zeros_like(p[0:1])
    for l in range(lg.shape[0]):
        run = run + p[l:l + 1]
        o_ref[l:l + 1, :] = run - p[0:1]


def _lower_bounds(logits):
    return pl.pallas_call(
        _lower_bounds_kernel,
        out_shape=jax.ShapeDtypeStruct(logits.shape, F32),
        name="hgrn_lower_bounds",
    )(logits)


def _softmax_rows(s):
    e = jnp.exp(s - jnp.max(s, axis=-1, keepdims=True))
    return e / jnp.sum(e, axis=-1, keepdims=True)


def _xattn_prompt_kernel(q_ref, k_ref, v_ref, o_ref):
    scale = X_HEAD_DIM ** -0.5
    for h in range(X_HEADS):
        sl = slice(h * X_HEAD_DIM, (h + 1) * X_HEAD_DIM)
        p = _softmax_rows(_bdot_nt(q_ref[:, sl], k_ref[:, sl]) * scale)
        o_ref[:, sl] = _bdot(p, v_ref[:, sl]).astype(o_ref.dtype)


def _xattn_prompt(q_arr, q_blk, mem_k, mem_v, nb, t):
    xdim = mem_k.shape[-1]
    n_mem = mem_k.shape[1]
    tq = _tile(t, 512, 16)
    nt = t // tq
    return pl.pallas_call(
        _xattn_prompt_kernel,
        out_shape=jax.ShapeDtypeStruct((nb * t, xdim), BF16),
        grid=(nb, nt),
        in_specs=[pl.BlockSpec((tq, xdim), lambda b, i: (b * nt + i, q_blk)),
                  pl.BlockSpec((None, n_mem, xdim), lambda b, i: (b, 0, 0)),
                  pl.BlockSpec((None, n_mem, xdim), lambda b, i: (b, 0, 0))],
        out_specs=pl.BlockSpec((tq, xdim), lambda b, i: (b * nt + i, 0)),
        compiler_params=_cparams("parallel", "parallel"),
        name="xattn_prompt",
    )(q_arr, mem_k, mem_v)


def _xattn_sample_kernel(q_ref, k_ref, v_ref, o_ref, *, bb):
    scale = X_HEAD_DIM ** -0.5
    nh = X_HEADS

    def fold(x):
        return x + pltpu.roll(x, nh, axis=0)

    def body(b, carry):
        q8 = q_ref[b] * scale
        s = jnp.sum(k_ref[b] * q8[None], axis=-1, keepdims=True)
        mx = jnp.broadcast_to(jnp.max(s, axis=0), (2 * nh, X_HEAD_DIM))
        mx = jnp.maximum(mx, pltpu.roll(mx, nh, axis=0))
        e = jnp.exp(s - mx[None])
        num = fold(jnp.sum(e * v_ref[b], axis=0))
        o_ref[b] = (num / fold(jnp.sum(e, axis=0)))[0:nh]
        return carry

    lax.fori_loop(0, bb, body, 0)


def _xattn_sample(q, cache_k, cache_v, layer):
    depth, nb, n_mem, nh, hd = cache_k.shape
    assert nh == X_HEADS and hd == X_HEAD_DIM and (n_mem * nh) % 8 == 0
    g = n_mem * nh // 8
    bb = 8
    q4 = q.reshape(nb, nh, hd)
    q8 = jnp.concatenate([q4, q4], axis=1)
    kv_spec = pl.BlockSpec((None, bb, g, 8, hd), lambda i: (layer, i, 0, 0, 0))
    out = pl.pallas_call(
        functools.partial(_xattn_sample_kernel, bb=bb),
        out_shape=jax.ShapeDtypeStruct((nb, nh, hd), F32),
        grid=(nb // bb,),
        in_specs=[pl.BlockSpec((bb, 2 * nh, hd), lambda i: (i, 0, 0)), kv_spec, kv_spec],
        out_specs=pl.BlockSpec((bb, nh, hd), lambda i: (i, 0, 0)),
        compiler_params=_cparams("parallel"),
        name="xattn_sample",
    )(q8, cache_k.reshape(depth, nb, g, 8, hd), cache_v.reshape(depth, nb, g, 8, hd))
    return out.reshape(nb, nh * hd)


HG_TB = 256
HG_C = 16


def _hgrn_prompt_kernel(q_ref, f_ref, i_ref, g_ref, lb_ref, og_ref, mix_ref, st_ref,
                        s_ref, qs_ref, kk_ref, bl_ref, o_ref, *, tb, c):
    t = pl.program_id(2)

    @pl.when(t == 0)
    def _():
        s_ref[...] = jnp.zeros_like(s_ref)

    lb = lb_ref[...]
    ff = f_ref[...]
    lg = jnp.log(jnp.maximum(lb + (1.0 - lb) * _sigmoid(ff), F_FLOOR)) * LOG2E
    kk_ref[...] = (1.0 - lb) * _sigmoid(-ff)
    qs_ref[...] = _silu(q_ref[...])
    shift = c.bit_length() - 1
    ri = _iota2((tb, tb), 0)
    ci = _iota2((tb, tb), 1)
    same_chunk = jnp.right_shift(ri, shift) == jnp.right_shift(ci, shift)
    lmat = jnp.where(same_chunk & (ci <= ri), 1.0, 0.0).astype(BF16)
    bl_ref[...] = _exact_ldot(lmat, lg)
    sub = SUBLANES
    rowi = _iota2((sub, 1), 0)

    nch = tb // c
    upd, dec = [], []
    for idx in range(nch):
        rows = slice(idx * c, (idx + 1) * c)
        bend = bl_ref[(idx + 1) * c - 1:(idx + 1) * c, :]
        upd.append(_bdot_tn(i_ref[rows, :], kk_ref[rows, :] * jnp.exp2(bend - bl_ref[rows, :])))
        dec.append(jnp.exp2(bend))
    states = [s_ref[...]]
    for idx in range(nch):
        states.append(states[idx] * dec[idx] + upd[idx])
    s_ref[...] = states[nch]
    for idx in range(nch):
        r0 = idx * c
        rows = slice(r0, r0 + c)
        o = _bdot_nt(qs_ref[rows, :] * jnp.exp2(bl_ref[rows, :]), states[idx])
        for h0 in range(0, c, sub):
            tile = slice(r0 + h0, r0 + h0 + sub)
            bt = bl_ref[tile, :]
            qt = qs_ref[tile, :]
            acc = o[h0:h0 + sub, :]
            for j in range(h0 + sub):
                r = r0 + j
                diff = bt - bl_ref[r:r + 1, :]
                if j >= h0:
                    a = jnp.sum(qt * kk_ref[r:r + 1, :] * jnp.exp2(jnp.minimum(diff, 0.0)),
                                axis=-1, keepdims=True)
                    a = jnp.where(rowi >= j - h0, a, 0.0)
                else:
                    a = jnp.sum(qt * kk_ref[r:r + 1, :] * jnp.exp2(diff), axis=-1, keepdims=True)
                acc = acc + a * i_ref[r:r + 1, :]
            o_ref[tile, :] = acc
    o = o_ref[...]
    y = o * lax.rsqrt(jnp.mean(o * o, axis=-1, keepdims=True) + RMS_EPS) * og_ref[...]
    mix_ref[...] = (y * _silu(g_ref[...])).astype(mix_ref.dtype)

    @pl.when(t == pl.num_programs(2) - 1)
    def _():
        st_ref[...] = s_ref[...].T


def _hgrn_prompt(p, lb, og, nb, t):
    mix = lb.shape[0]
    hd = A_HEAD_DIM
    nh = mix // hd
    tb = _tile(t, HG_TB, HG_C)
    nt = t // tb
    col = lambda off: pl.BlockSpec((tb, hd), lambda b, h, i: (b * nt + i, off + h))
    vec = pl.BlockSpec((1, hd), lambda b, h, i: (0, h))
    return pl.pallas_call(
        functools.partial(_hgrn_prompt_kernel, tb=tb, c=HG_C),
        out_shape=(jax.ShapeDtypeStruct((nb * t, mix), BF16),
                   jax.ShapeDtypeStruct((nb, nh, hd, hd), F32)),
        grid=(nb, nh, nt),
        in_specs=[col(0), col(nh), col(2 * nh), col(3 * nh), vec, vec],
        out_specs=(pl.BlockSpec((tb, hd), lambda b, h, i: (b * nt + i, h)),
                   pl.BlockSpec((None, None, hd, hd), lambda b, h, i: (b, h, 0, 0))),
        scratch_shapes=[pltpu.VMEM((hd, hd), F32)] + [pltpu.VMEM((tb, hd), F32)] * 4,
        compiler_params=_cparams("parallel", "parallel", "arbitrary"),
        name="hgrn_prompt",
    )(p, p, p, p, lb.reshape(1, mix), og.reshape(1, mix))


def _layer_view(so_ref, layer, first):
    if not first:
        return so_ref
    for l in range(so_ref.shape[0]):
        if l != layer:
            so_ref[l] = jnp.zeros(so_ref.shape[1:], so_ref.dtype)
    return so_ref.at[layer]


def _hgrn_sample_kernel(p_ref, lb_ref, og_ref, s_ref, *rest, bb, layer, first):
    mix_ref, so_ref = rest[-2:]
    so_ref = _layer_view(so_ref, layer, first)
    hd = s_ref.shape[-1]
    eye = (_iota2((hd, hd), 0) == _iota2((hd, hd), 1)).astype(F32)
    lb = lb_ref[...]
    col = lambda r: jnp.sum(eye[None] * r, axis=-1, keepdims=True)

    def body(b, carry):
        q, f, iv, g = p_ref[b, 0], p_ref[b, 1], p_ref[b, 2], p_ref[b, 3]
        dec = jnp.maximum(lb + (1.0 - lb) * _sigmoid(f), F_FLOOR)
        k = (1.0 - lb) * _sigmoid(-f)
        sn = s_ref[b] * col(dec) + col(k) * iv
        so_ref[b] = sn
        o = jnp.sum(sn * col(_silu(q)), axis=1, keepdims=True)
        y = o * lax.rsqrt(jnp.mean(o * o, axis=-1, keepdims=True) + RMS_EPS) * og_ref[...]
        mix_ref[b] = y * _silu(g)
        return carry

    lax.fori_loop(0, bb, body, 0)


def _stacked_state_io(states_shape, block, index_map, stacked):
    out_shape = jax.ShapeDtypeStruct(states_shape, F32)
    if stacked is None:
        all_layers = pl.BlockSpec((states_shape[0],) + tuple(block[1:]),
                                  lambda *idx: (0,) + tuple(index_map(*idx)[1:]))
        return out_shape, all_layers, [], []
    return out_shape, pl.BlockSpec(block, index_map), [pl.BlockSpec(memory_space=pl.ANY)], [stacked]


def _hgrn_sample(p, lb, og, states, layer, stacked):
    mix = lb.shape[0]
    hd = A_HEAD_DIM
    nh = mix // hd
    nb = p.shape[0]
    bb = 4
    p5 = p[:, :4 * mix].reshape(nb, 4, nh, 1, hd)
    vec = pl.BlockSpec((nh, 1, hd), lambda i: (0, 0, 0))
    st_block = (None, bb, nh, hd, hd)
    st_map = lambda i: (layer, i, 0, 0, 0)
    so_shape, so_spec, alias_specs, alias_args = _stacked_state_io(states.shape, st_block, st_map, stacked)
    out, new_states = pl.pallas_call(
        functools.partial(_hgrn_sample_kernel, bb=bb, layer=layer, first=stacked is None),
        out_shape=(jax.ShapeDtypeStruct((nb, nh, 1, hd), F32), so_shape),
        grid=(nb // bb,),
        in_specs=[pl.BlockSpec((bb, 4, nh, 1, hd), lambda i: (i, 0, 0, 0, 0)), vec, vec,
                  pl.BlockSpec(st_block, st_map)] + alias_specs,
        out_specs=(pl.BlockSpec((bb, nh, 1, hd), lambda i: (i, 0, 0, 0)), so_spec),
        input_output_aliases={4: 1} if alias_args else {},
        compiler_params=_cparams("parallel"),
        name="hgrn_sample",
    )(p5, lb.reshape(nh, 1, hd), og.reshape(nh, 1, hd), states, *alias_args)
    return out.reshape(nb, mix), new_states


RW_TR = 128
RW_TB = 256
RW_C = 64
RW_PAIRS = 3


def _softplus(x):
    return jnp.maximum(x, 0.0) + jnp.log(1.0 + jnp.exp(-jnp.abs(x)))


def _seg_sum(x, seg):
    lo = _iota2(x.shape, x.ndim - 1) < seg
    s0 = jnp.sum(jnp.where(lo, x, 0.0), axis=-1, keepdims=True)
    s1 = jnp.sum(jnp.where(lo, 0.0, x), axis=-1, keepdims=True)
    return jnp.where(lo, s0, s1)


def _rwkv_prep_kernel(p_ref, prev_ref, mu_ref, w0_ref, a0_ref, kkw_ref, ka_ref, w2_ref, a2_ref, g2_ref,
                      r_o, k_o, v_o, kk_o, kka_o, w_o, g_o, *, mix):
    p = p_ref[...]
    xs = p + (prev_ref[...] - p) * mu_ref[...]
    r = xs[:, :mix]
    k = xs[:, mix:2 * mix]
    v = xs[:, 2 * mix:3 * mix]
    la = xs[:, 3 * mix:3 * mix + LANES]
    gl = xs[:, 3 * mix + LANES:3 * mix + 3 * LANES]
    w_raw = -_softplus(-(w0_ref[...] + _bdot(jnp.tanh(la), w2_ref[...]))) - 0.5
    a = _sigmoid(a0_ref[...] + _bdot(la, a2_ref[...]))
    r_o[...] = r
    v_o[...] = v
    w_o[...] = jnp.exp(-jnp.exp(w_raw))
    g_o[...] = _bdot(_sigmoid(gl), g2_ref[...])
    k_o[...] = k * (1.0 + (a - 1.0) * ka_ref[...])
    for j in range(mix // LANES):
        sl = slice(j * LANES, (j + 1) * LANES)
        kk = k[:, sl] * kkw_ref[:, sl]
        kk = kk / jnp.maximum(jnp.sqrt(_seg_sum(kk * kk, B_HEAD_DIM)), 1e-12)
        kk_o[:, sl] = kk
        kka_o[:, sl] = kk * a[:, sl]


def _rwkv_prep(p, prev, params):
    mu, w0, a0, kkw, ka, w2p, a2p, g2p = params
    mix = w0.shape[-1]
    rows, ncol = p.shape
    tr = _tile(rows, RW_TR, 8)
    row_in = pl.BlockSpec((tr, ncol), lambda i: (i, 0))
    row_out = pl.BlockSpec((tr, mix), lambda i: (i, 0))
    consts = [mu, w0, a0, kkw, ka, w2p, a2p, g2p]
    return pl.pallas_call(
        functools.partial(_rwkv_prep_kernel, mix=mix),
        out_shape=tuple(jax.ShapeDtypeStruct((rows, mix), F32) for _ in range(7)),
        grid=(rows // tr,),
        in_specs=[row_in, row_in] + [pl.BlockSpec(x.shape, lambda i: (0, 0)) for x in consts],
        out_specs=tuple(row_out for _ in range(7)),
        compiler_params=_cparams("parallel"),
        name="rwkv_prep",
    )(p, prev, *consts)


def _rwkv_scan_kernel(pr_ref, pk_ref, pv_ref, pl_ref, mur_ref, muk_ref, muv_ref, mul_ref,
                      w0_ref, a0_ref, kkw_ref, ka_ref, w2_ref, a2_ref, g2_ref,
                      rk_ref, lg_ref, lbias_ref, mix_ref, st_ref,
                      s_ref, cr_ref, ck_ref, cv_ref, cl_ref,
                      r_ref, k_ref, v_ref, kk_ref, kka_ref, lw_ref, g_ref, *, tb, c, pp):
    n2 = 2 * c
    assert n2 == LANES
    t = pl.program_id(2)

    @pl.when(t == 0)
    def _():
        for ref in (s_ref, cr_ref, ck_ref, cv_ref, cl_ref):
            ref[...] = jnp.zeros_like(ref)

    def shifted(p_ref, carry_ref, mu_ref):
        p = p_ref[...]
        prev = jnp.where(_iota2(p.shape, 0) == 0, carry_ref[...], pltpu.roll(p, 1, axis=0))
        carry_ref[...] = p[tb - 1:tb, :]
        return p + (prev - p) * mu_ref[...]

    xl = shifted(pl_ref, cl_ref, mul_ref)
    la = xl[:, 0:LANES]
    gl = xl[:, LANES:3 * LANES]
    w_raw = -_softplus(-(w0_ref[...] + _bdot(jnp.tanh(la), w2_ref[...]))) - 0.5
    lw_ref[...] = -jnp.exp(w_raw)
    a = _sigmoid(a0_ref[...] + _bdot(la, a2_ref[...]))
    g_ref[...] = _bdot(_sigmoid(gl), g2_ref[...])
    r_ref[...] = shifted(pr_ref, cr_ref, mur_ref)
    v_ref[...] = shifted(pv_ref, cv_ref, muv_ref)
    k_raw = shifted(pk_ref, ck_ref, muk_ref)
    k_ref[...] = k_raw * (1.0 + (a - 1.0) * ka_ref[...])
    for p in range(pp):
        sl = slice(p * LANES, (p + 1) * LANES)
        kk_p = k_raw[:, sl] * kkw_ref[:, sl]
        kk_p = kk_p / jnp.maximum(jnp.sqrt(_seg_sum(kk_p * kk_p, B_HEAD_DIM)), 1e-12)
        kk_ref[:, sl] = kk_p
        kka_ref[:, sl] = kk_p * a[:, sl]

    ri = _iota2((n2, n2), 0)
    ci = _iota2((n2, n2), 1)
    blk = lambda x, n: jnp.right_shift(x, n.bit_length() - 1)
    same_head = blk(ri, c) == blk(ci, c)
    strict = same_head & (ri > ci)
    incl = same_head & (ri >= ci)
    eye = (ri == ci).astype(F32)
    tril = (_iota2((c, c), 0) >= _iota2((c, c), 1)).astype(BF16)
    head0 = _iota2((c, LANES), 1) < B_HEAD_DIM
    hd = float(B_HEAD_DIM)

    def sibling(n):
        return (blk(ri, 2 * n) == blk(ci, 2 * n)) & (blk(ri, n) > blk(ci, n))

    def stack(x):
        return jnp.concatenate([jnp.where(head0, x, 0.0), jnp.where(head0, 0.0, x)], axis=0)

    nch = tb // c
    inst = [(p, ch) for p in range(pp) for ch in range(nch)]
    each = lambda f, *lists: [f(*xs) for xs in zip(*lists)]
    win = lambda ref: [ref[ch * c:(ch + 1) * c, p * LANES:(p + 1) * LANES] for p, ch in inst]
    r, k, v, kk, kka, lw = (win(x) for x in (r_ref, k_ref, v_ref, kk_ref, kka_ref, lw_ref))
    cs = each(lambda x: _exact_ldot(tril, x), lw)
    gend = each(lambda x: jnp.exp(x[c - 1:c, :]), cs)
    ginv = each(lambda x: jnp.exp(-x), cs)
    ph = each(jnp.multiply, kka, ginv)
    kh = each(jnp.multiply, k, ginv)
    kk_m = each(lambda a, b, d: stack(a * jnp.exp(b - d)), kk, cs, lw)
    r_m = each(lambda a, b: stack(a * jnp.exp(b)), r, cs)
    v_m = each(stack, v)
    lhs = each(lambda a, b: jnp.concatenate([a, b], axis=0), kk_m, r_m)
    gp = each(lambda a, b: _bdot_nt(a, stack(b)), lhs, ph)
    gk = each(lambda a, b: _bdot_nt(a, stack(b)), lhs, kh)
    a_mat = each(lambda g: jnp.where(strict, g[0:n2], 0.0), gp)
    b_mat = each(lambda g: jnp.where(strict, g[0:n2], 0.0), gk)
    rp = each(lambda g: jnp.where(incl, g[n2:], 0.0), gp)
    rk = each(lambda g: jnp.where(incl, g[n2:], 0.0), gk)
    bv = each(_bdot, b_mat, v_m)
    rkv = each(_bdot, rk, v_m)
    t_m = each(lambda a: eye - jnp.where(blk(ri, 2) == blk(ci, 2), a, 0.0), a_mat)
    n = 2
    while n < c:
        y = each(lambda a, tm: _bdot(jnp.where(sibling(n), a, 0.0), tm), a_mat, t_m)
        t_m = each(lambda tm, yy: tm - _bdot(tm, yy), t_m, y)
        n *= 2
    ku = each(lambda tm, a, b: _bdot(tm, jnp.concatenate([a, b], axis=1)), t_m, kk_m, bv)
    rpku = each(_bdot, rp, ku)
    ml = each(lambda a, b, g: _bdot_tn(a, stack(b * g)), ku, ph, gend)
    vk = each(lambda a, b, g: _bdot_tn(a, stack(b * g)), v_m, kh, gend)
    r_t = each(lambda a, b: a - b[:, 0:n2], r_m, rpku)
    o_0 = each(lambda a, b: a - b[:, n2:], rkv, rpku)
    cst = each(lambda a, b: a - b[n2:], vk, ml)
    s = [s_ref[p] for p in range(pp)]
    o2 = [None] * len(inst)
    for ch in range(nch):
        for p in range(pp):
            i = p * nch + ch
            o2[i] = _bdot_nt(r_t[i], s[p]) + o_0[i]
            s[p] = s[p] * gend[i] - _bdot(s[p], ml[i][0:n2]) + cst[i]
    for p in range(pp):
        s_ref[p] = s[p]
    for i, (p, ch) in enumerate(inst):
        rows = slice(ch * c, (ch + 1) * c)
        cols = slice(p * LANES, (p + 1) * LANES)
        o = o2[i][0:c, :] + o2[i][c:n2, :]
        d = o - _seg_sum(o, B_HEAD_DIM) / hd
        on = (d * lax.rsqrt(_seg_sum(d * d, B_HEAD_DIM) / hd + GN_EPS) * lg_ref[:, cols]
              + lbias_ref[:, cols])
        bonus = _seg_sum(r[i] * k[i] * rk_ref[:, cols], B_HEAD_DIM) * v[i]
        mix_ref[rows, cols] = ((on + bonus) * g_ref[rows, cols]).astype(mix_ref.dtype)

    @pl.when(t == pl.num_programs(2) - 1)
    def _():
        for p in range(pp):
            s = s_ref[p]
            st_ref[2 * p] = s[0:c, 0:c]
            st_ref[2 * p + 1] = s[c:n2, c:n2]


def _rwkv_scan(p_main, p_lora, params, rk, lg, lbias, nb, t):
    mu, w0, a0, kkw, ka, w2p, a2p, g2p = params
    mix = rk.shape[-1]
    hd = B_HEAD_DIM
    nh = mix // hd
    pp = RW_PAIRS
    wc = pp * LANES
    nblk = mix // wc
    nl = p_lora.shape[1]
    assert nh % (2 * pp) == 0 and p_main.shape[1] == 3 * mix and mu.shape[1] >= 3 * mix + nl
    tb = _tile(t, RW_TB, RW_C)
    nt = t // tb
    col = lambda off: pl.BlockSpec((tb, wc), lambda b, h, i: (b * nt + i, off + h))
    vec = lambda off=0: pl.BlockSpec((1, wc), lambda b, h, i: (0, off + h))
    mat = lambda x: pl.BlockSpec((x.shape[0], wc), lambda b, h, i: (0, h))
    mu_l = lax.slice(mu, (0, 3 * mix), (1, 3 * mix + nl))
    vec1 = lambda x: x.reshape(1, mix)
    return pl.pallas_call(
        functools.partial(_rwkv_scan_kernel, tb=tb, c=RW_C, pp=pp),
        out_shape=(jax.ShapeDtypeStruct((nb * t, mix), BF16),
                   jax.ShapeDtypeStruct((nb, nh, hd, hd), F32)),
        grid=(nb, nblk, nt),
        in_specs=[col(0), col(nblk), col(2 * nblk),
                  pl.BlockSpec((tb, nl), lambda b, h, i: (b * nt + i, 0)),
                  vec(0), vec(nblk), vec(2 * nblk), pl.BlockSpec((1, nl), lambda b, h, i: (0, 0)),
                  vec(), vec(), vec(), vec(), mat(w2p), mat(a2p), mat(g2p), vec(), vec(), vec()],
        out_specs=(col(0), pl.BlockSpec((None, 2 * pp, hd, hd), lambda b, h, i: (b, h, 0, 0))),
        scratch_shapes=([pltpu.VMEM((pp, LANES, LANES), F32)] + [pltpu.VMEM((1, wc), F32)] * 3
                        + [pltpu.VMEM((1, nl), F32)] + [pltpu.VMEM((tb, wc), F32)] * 7),
        compiler_params=_cparams("parallel", "parallel", "arbitrary"),
        name="rwkv_scan",
    )(p_main, p_main, p_main, p_lora, mu, mu, mu, mu_l, w0, a0, kkw, ka, w2p, a2p, g2p,
      vec1(rk), vec1(lg), vec1(lbias))


def _rwkv_sample_kernel(r_ref, k_ref, v_ref, kk_ref, kka_ref, w_ref, g_ref, rk_ref, lg_ref, lbias_ref,
                        s_ref, *rest, layer, first):
    o_ref, so_ref, vt_ref, ot_ref = rest[-4:]
    so_ref = _layer_view(so_ref, layer, first)
    hd = s_ref.shape[1]
    r = r_ref[...]
    k = k_ref[...]
    r_t, k_t, w_t, kka_t = r.T, k.T, w_ref[...].T, kka_ref[...].T
    nkk_t = -(kk_ref[...].T)
    vt_ref[...] = v_ref[...].T
    bon_t = (r * k * rk_ref[...]).T
    for h in range(2):
        sl = slice(h * hd, (h + 1) * hd)
        nkk, kka, w, kf, rr = nkk_t[sl], kka_t[sl], w_t[sl], k_t[sl], r_t[sl]

        def body(i, carry, h=h, nkk=nkk, kka=kka, w=w, kf=kf, rr=rr):
            row = pl.ds(h * hd + i, 1)
            sv = s_ref[h, i]
            sa = jnp.sum(sv * nkk, axis=0, keepdims=True)
            sn = sv * w + sa * kka + vt_ref[row, :] * kf
            so_ref[h, i] = sn
            ot_ref[row, :] = jnp.sum(sn * rr, axis=0, keepdims=True)
            return carry

        lax.fori_loop(0, hd, body, 0, unroll=4)
    dn, bonus = [], []
    for h in range(2):
        sl = slice(h * hd, (h + 1) * hd)
        o = ot_ref[sl, :]
        d = o - jnp.mean(o, axis=0, keepdims=True)
        dn.append(d * lax.rsqrt(jnp.mean(d * d, axis=0, keepdims=True) + GN_EPS))
        bonus.append(jnp.sum(bon_t[sl], axis=0, keepdims=True) * vt_ref[sl, :])
    dn = jnp.concatenate(dn, axis=0).T
    bonus = jnp.concatenate(bonus, axis=0).T
    o_ref[...] = (dn * lg_ref[...] + lbias_ref[...] + bonus) * g_ref[...]


def _rwkv_sample(prep, rk, lg, lbias, states_t, layer, stacked):
    mix = rk.shape[-1]
    hd = B_HEAD_DIM
    nb = prep[0].shape[0]
    assert nb == LANES
    col = pl.BlockSpec((nb, LANES), lambda h: (0, h))
    vec = pl.BlockSpec((1, LANES), lambda h: (0, h))
    st_block = (None, 2, hd, hd, nb)
    st_map = lambda h: (layer, h, 0, 0, 0)
    so_shape, so_spec, alias_specs, alias_args = _stacked_state_io(states_t.shape, st_block, st_map, stacked)
    return pl.pallas_call(
        functools.partial(_rwkv_sample_kernel, layer=layer, first=stacked is None),
        out_shape=(jax.ShapeDtypeStruct((nb, mix), F32), so_shape),
        grid=(mix // LANES,),
        in_specs=[col] * 7 + [vec] * 3 + [pl.BlockSpec(st_block, st_map)] + alias_specs,
        out_specs=(col, so_spec),
        scratch_shapes=[pltpu.VMEM((LANES, nb), F32)] * 2,
        input_output_aliases={11: 1} if alias_args else {},
        compiler_params=_cparams("parallel"),
        name="rwkv_sample",
    )(*prep, rk.reshape(1, mix), lg.reshape(1, mix), lbias.reshape(1, mix), states_t, *alias_args)


def kernel(x_prompt, x_sample, cache_mem_k, cache_mem_v, state_hgrn, state_rwkv, state_rwkv_shift,
           mem_prompt, attn_norm_g, mlp_norm_g, final_norm_g, mem_norm_g, wk_mem, wv_mem,
           a_w_in, a_w_out, a_lb_logits, a_onorm_g,
           b_w_in, b_w_out, b_mu, b_w0, b_w2, b_a0, b_a2, b_g2, b_k_k, b_k_a, b_r_k, b_lnx_g, b_lnx_b,
           mlp_w1, mlp_w2):
    nb, t, d = x_prompt.shape
    ns = x_sample.shape[0]
    depth = attn_norm_g.shape[0]
    mix = a_onorm_g.shape[-1]
    xdim = d - mix
    n_mem = mem_prompt.shape[1]
    b_cols = b_mu.shape[-1]
    b_pad = -(-b_cols // LANES) * LANES
    lora0 = 3 * mix

    lbs = _lower_bounds(a_lb_logits)

    m = _rmsnorm(mem_prompt.reshape(nb * n_mem, d), mem_norm_g, BF16)
    mem_k = [_matmul(m, wk_mem, layer=l, name="mem_k") for l in range(depth)]
    mem_v = [_matmul(m, wv_mem, layer=l, name="mem_v") for l in range(depth)]
    state_rwkv_t = jnp.transpose(state_rwkv, (0, 2, 3, 4, 1))
    b_w_in_t = jnp.transpose(b_w_in, (0, 2, 1))

    rp = nb * t
    x = jnp.concatenate([x_prompt.reshape(rp, d), x_sample.reshape(ns, d)], axis=0)
    rows_of = lambda a: a[rp:rp + ns]
    hgrn_p, rwkv_p, shift_p, shift_s = [], [], [], []
    hgrn_s = rwkv_s_t = None
    for layer in range(depth):
        j = layer // 2
        g_attn = attn_norm_g[layer]
        h = _rmsnorm(x, g_attn, BF16)
        if layer % 2 == 0:
            p = _matmul(h, a_w_in, layer=j, name="a_in")
            mix_p, st_p = _hgrn_prompt(p, lbs[j], a_onorm_g[j], nb, t)
            mix_s, hgrn_s = _hgrn_sample(rows_of(p), lbs[j], a_onorm_g[j], state_hgrn, j, hgrn_s)
            hgrn_p.append(st_p)
            q_arr, q_blk = p, (4 * mix) // xdim
            qs = rows_of(p)[:, 4 * mix:]
            w_out = a_w_out
        else:
            zeros = lambda n: jnp.zeros((n, mix), F32)
            params = (
                jnp.pad(b_mu[j], (0, b_pad - b_cols)).reshape(1, b_pad),
                b_w0[j].reshape(1, mix), b_a0[j].reshape(1, mix),
                b_k_k[j].reshape(1, mix), b_k_a[j].reshape(1, mix),
                jnp.concatenate([b_w2[j], zeros(LANES - B_DECAY_LORA)], axis=0),
                jnp.concatenate([zeros(B_DECAY_LORA), b_a2[j]], axis=0),
                jnp.concatenate([b_g2[j], zeros(2 * LANES - B_GATE_LORA)], axis=0),
            )
            assert lora0 + B_DECAY_LORA + B_AAA_LORA + B_GATE_LORA == b_cols
            w_q = lax.slice(b_w_in_t, (j, b_cols, 0), (j + 1, b_w_in_t.shape[1], d)).reshape(xdim, d)
            p = _matmul(h, b_w_in_t, layer=j, n_out=lora0, wt=True, name="b_in")
            p_lora = _matmul(h, b_w_in_t, layer=j, n_out=b_pad - lora0, n_off=lora0, wt=True,
                             name="b_lora")
            q_arr, q_blk = _matmul(h, w_q, wt=True, name="b_q"), 0
            qs = rows_of(q_arr)
            ps = jnp.concatenate([rows_of(p), rows_of(p_lora)], axis=1)
            ps_prev = _matmul(state_rwkv_shift[j], b_w_in_t, layer=j, n_out=b_pad, wt=True,
                              name="b_in_shift")
            rk, lg, lbias = b_r_k[j].reshape(mix), b_lnx_g[j], b_lnx_b[j]
            mix_p, st_p = _rwkv_scan(p, p_lora, params, rk, lg, lbias, nb, t)
            mix_s, rwkv_s_t = _rwkv_sample(_rwkv_prep(ps, ps_prev, params), rk, lg, lbias,
                                           state_rwkv_t, j, rwkv_s_t)
            rwkv_p.append(st_p)
            shift_p.append(_rmsnorm(x[t - 1:rp:t], g_attn, F32))
            shift_s.append(_rmsnorm(x, g_attn, F32, rp, ns))
            w_out = b_w_out
        xo_p = _xattn_prompt(q_arr, q_blk, mem_k[layer].reshape(nb, n_mem, xdim),
                             mem_v[layer].reshape(nb, n_mem, xdim), nb, t)
        xo_s = _xattn_sample(qs, cache_mem_k, cache_mem_v, layer)
        x = _matmul(mix_p, w_out, layer=j, a2=xo_p, res=x, res_row0=0, name="out_prompt")
        x = _matmul(mix_s, w_out, layer=j, a2=xo_s, res=x, res_row0=rp, name="out_sample")
        u = _matmul(_rmsnorm(x, mlp_norm_g[layer], BF16), mlp_w1, layer=layer, act="relu2",
                    out_dtype=BF16, tn_cap=1024, name="mlp1")
        x = _matmul(u, mlp_w2, layer=layer, res=x, tm_cap=MLP2_TM, tn_cap=256, tk_cap=u.shape[1],
                    name="mlp2")

    y_p = _rmsnorm(x, final_norm_g, F32, 0, rp).reshape(nb, t, d)
    y_s = _rmsnorm(x, final_norm_g, F32, rp, ns).reshape(ns, 1, d)
    kv_shape = (depth, nb, n_mem, X_HEADS, X_HEAD_DIM)
    return (y_p, y_s, jnp.stack(mem_k).reshape(kv_shape), jnp.stack(mem_v).reshape(kv_shape),
            jnp.stack(hgrn_p), jnp.stack(rwkv_p), jnp.stack(shift_p),
            hgrn_s, jnp.transpose(rwkv_s_t, (0, 4, 1, 2, 3)), jnp.stack(shift_s))
```

```python
import functools

import jax
import jax.numpy as jnp
from jax import lax
from jax.experimental import pallas as pl
from jax.experimental.pallas import tpu as pltpu

F32 = jnp.float32
BF16 = jnp.bfloat16

RMS_EPS = 1e-6
GN_EPS = 64e-5
F_FLOOR = 1e-30
X_HEADS = 4
X_HEAD_DIM = 128
A_HEAD_DIM = 128
B_HEAD_DIM = 64
B_DECAY_LORA = 64
B_AAA_LORA = 64
B_GATE_LORA = 224

LANES = 128
SUBLANES = 8
LOG2E = 1.4426950408889634
VMEM_LIMIT = 60 * 1024 * 1024
MM_TM = 2080
MLP2_TM = 1040


def _cparams(*sem):
    return pltpu.CompilerParams(dimension_semantics=sem, vmem_limit_bytes=VMEM_LIMIT)


def _tile(n, cap, mult=8):
    if n <= cap:
        return n
    best = None
    for d in range(mult, cap + 1, mult):
        if n % d == 0:
            best = d
    assert best is not None, (n, cap, mult)
    return best


def _bdot(a, b):
    return jnp.dot(a.astype(BF16), b.astype(BF16), preferred_element_type=F32)


def _bdot_nt(a, b):
    return lax.dot_general(a.astype(BF16), b.astype(BF16), (((1,), (1,)), ((), ())),
                           preferred_element_type=F32)


def _bdot_tn(a, b):
    return lax.dot_general(a.astype(BF16), b.astype(BF16), (((0,), (0,)), ((), ())),
                           preferred_element_type=F32)


def _split3(x):
    hi = x.astype(BF16)
    r1 = x - hi.astype(F32)
    mid = r1.astype(BF16)
    lo = (r1 - mid.astype(F32)).astype(BF16)
    return hi, mid, lo


def _exact_ldot(m01, x):
    hi, mid, lo = _split3(x)
    d = lambda p: jnp.dot(m01, p, preferred_element_type=F32)
    return d(hi) + d(mid) + d(lo)


def _sigmoid(x):
    return 1.0 / (1.0 + jnp.exp(-x))


def _silu(x):
    return x * _sigmoid(x)


def _iota2(shape, axis):
    return lax.broadcasted_iota(jnp.int32, shape, axis)


def _rmsnorm_kernel(x_ref, g_ref, o_ref):
    x = x_ref[...]
    y = x * lax.rsqrt(jnp.mean(x * x, axis=-1, keepdims=True) + RMS_EPS)
    o_ref[...] = (y * g_ref[...]).astype(o_ref.dtype)


def _rmsnorm(x, g, out_dtype, row0=0, nrows=None):
    d = x.shape[1]
    m = x.shape[0] - row0 if nrows is None else nrows
    tm = _tile(m, 512, 16)
    assert row0 % tm == 0
    off = row0 // tm
    return pl.pallas_call(
        _rmsnorm_kernel,
        out_shape=jax.ShapeDtypeStruct((m, d), out_dtype),
        grid=(m // tm,),
        in_specs=[pl.BlockSpec((tm, d), lambda i: (i + off, 0)),
                  pl.BlockSpec((1, d), lambda i: (0, 0))],
        out_specs=pl.BlockSpec((tm, d), lambda i: (i, 0)),
        compiler_params=_cparams("parallel"),
        name="rmsnorm",
    )(x, g.reshape(1, d))


def _matmul_kernel(*refs, k1, nk, act, has_res, has_a2, wt):
    it = iter(refs)
    a1_ref = next(it)
    a2_ref = next(it) if has_a2 else None
    w_ref = next(it)
    res_ref = next(it) if has_res else None
    o_ref = next(it)

    if nk == 1:
        if wt:
            assert not has_a2
            acc = _bdot_nt(a1_ref[...], w_ref[...])
        else:
            acc = _bdot(a1_ref[...], w_ref[0:k1, :])
        if has_a2:
            acc = acc + _bdot(a2_ref[...], w_ref[k1:, :])
        if act == "relu2":
            acc = jnp.square(jnp.maximum(acc, 0.0))
        if has_res:
            acc = res_ref[...] + acc
        o_ref[...] = acc.astype(o_ref.dtype)
        return

    k = pl.program_id(2)
    part = _bdot(a1_ref[...], w_ref[...])

    @pl.when(k == 0)
    def _():
        o_ref[...] = (res_ref[...] + part) if has_res else part

    @pl.when(k > 0)
    def _():
        o_ref[...] += part


def _matmul(a1, w, *, layer=None, a2=None, res=None, res_row0=None, act=None, out_dtype=F32,
            n_out=None, n_off=0, wt=False, tm_cap=MM_TM, tn_cap=512, tk_cap=2048, name="matmul"):
    m, k1 = a1.shape
    k2 = a2.shape[1] if a2 is not None else 0
    kdim = k1 + k2
    n = w.shape[-2 if wt else -1] if n_out is None else n_out
    tm = _tile(m, tm_cap, 16 if a1.dtype == BF16 or out_dtype == BF16 else 8)
    tn = _tile(n, tn_cap, LANES)
    tk = _tile(kdim, tk_cap, LANES)
    nk = kdim // tk
    if nk > 1:
        assert a2 is None and act is None and out_dtype == F32 and not wt
    assert w.shape[-1 if wt else -2] == kdim and n_off % tn == 0
    j_off = n_off // tn
    w_block = (tn, tk) if wt else (tk, tn)
    w_idx = (lambda k, j: (j + j_off, k)) if wt else (lambda k, j: (k, j + j_off))
    if layer is None:
        w_spec = pl.BlockSpec(w_block, lambda i, j, k: w_idx(k, j))
    else:
        w_spec = pl.BlockSpec((None,) + w_block, lambda i, j, k: (layer,) + w_idx(k, j))
    in_specs = [pl.BlockSpec((tm, min(tk, k1)), lambda i, j, k: (i, k))]
    args = [a1]
    if a2 is not None:
        in_specs.append(pl.BlockSpec((tm, k2), lambda i, j, k: (i, 0)))
        args.append(a2)
    in_specs.append(w_spec)
    args.append(w)
    out_shape = jax.ShapeDtypeStruct((m, n), out_dtype)
    out_spec = pl.BlockSpec((tm, tn), lambda i, j, k: (i, j))
    aliases = {}
    if res is not None:
        if res_row0 is not None:
            assert res_row0 % tm == 0 and res.shape[1] == n and res.dtype == out_dtype
            i_off = res_row0 // tm
            out_spec = pl.BlockSpec((tm, tn), lambda i, j, k: (i + i_off, j))
            out_shape = jax.ShapeDtypeStruct(res.shape, out_dtype)
            aliases = {len(args): 0}
        in_specs.append(out_spec)
        args.append(res)
    kern = functools.partial(_matmul_kernel, k1=k1, nk=nk, act=act,
                             has_res=res is not None, has_a2=a2 is not None, wt=wt)
    return pl.pallas_call(
        kern,
        out_shape=out_shape,
        grid=(m // tm, n // tn, nk),
        in_specs=in_specs,
        out_specs=out_spec,
        input_output_aliases=aliases,
        compiler_params=_cparams("parallel", "parallel", "arbitrary"),
        name=name,
    )(*args)


def _lower_bounds_kernel(lg_ref, o_ref):
    lg = lg_ref[...]
    e = jnp.exp(lg - jnp.max(lg, axis=0, keepdims=True))
    p = e / jnp.sum(e, axis=0, keepdims=True)
    run = jnp.zeros_like(p[0:1])
    for l in range(lg.shape[0]):
        run = run + p[l:l + 1]
        o_ref[l:l + 1, :] = run - p[0:1]


def _lower_bounds(logits):
    return pl.pallas_call(
        _lower_bounds_kernel,
        out_shape=jax.ShapeDtypeStruct(logits.shape, F32),
        name="hgrn_lower_bounds",
    )(logits)


def _softmax_rows(s):
    e = jnp.exp(s - jnp.max(s, axis=-1, keepdims=True))
    return e / jnp.sum(e, axis=-1, keepdims=True)


def _xattn_prompt_kernel(q_ref, k_ref, v_ref, o_ref):
    scale = X_HEAD_DIM ** -0.5
    for h in range(X_HEADS):
        sl = slice(h * X_HEAD_DIM, (h + 1) * X_HEAD_DIM)
        p = _softmax_rows(_bdot_nt(q_ref[:, sl], k_ref[:, sl]) * scale)
        o_ref[:, sl] = _bdot(p, v_ref[:, sl]).astype(o_ref.dtype)


def _xattn_prompt(q_arr, q_blk, mem_k, mem_v, nb, t):
    xdim = mem_k.shape[-1]
    n_mem = mem_k.shape[1]
    tq = _tile(t, 512, 16)
    nt = t // tq
    return pl.pallas_call(
        _xattn_prompt_kernel,
        out_shape=jax.ShapeDtypeStruct((nb * t, xdim), BF16),
        grid=(nb, nt),
        in_specs=[pl.BlockSpec((tq, xdim), lambda b, i: (b * nt + i, q_blk)),
                  pl.BlockSpec((None, n_mem, xdim), lambda b, i: (b, 0, 0)),
                  pl.BlockSpec((None, n_mem, xdim), lambda b, i: (b, 0, 0))],
        out_specs=pl.BlockSpec((tq, xdim), lambda b, i: (b * nt + i, 0)),
        compiler_params=_cparams("parallel", "parallel"),
        name="xattn_prompt",
    )(q_arr, mem_k, mem_v)


def _xattn_sample_kernel(q_ref, k_ref, v_ref, o_ref, *, bb):
    scale = X_HEAD_DIM ** -0.5
    nh = X_HEADS

    def fold(x):
        return x + pltpu.roll(x, nh, axis=0)

    def body(b, carry):
        q8 = q_ref[b] * scale
        s = jnp.sum(k_ref[b] * q8[None], axis=-1, keepdims=True)
        mx = jnp.broadcast_to(jnp.max(s, axis=0), (2 * nh, X_HEAD_DIM))
        mx = jnp.maximum(mx, pltpu.roll(mx, nh, axis=0))
        e = jnp.exp(s - mx[None])
        num = fold(jnp.sum(e * v_ref[b], axis=0))
        o_ref[b] = (num / fold(jnp.sum(e, axis=0)))[0:nh]
        return carry

    lax.fori_loop(0, bb, body, 0)


def _xattn_sample(q, cache_k, cache_v, layer):
    depth, nb, n_mem, nh, hd = cache_k.shape
    assert nh == X_HEADS and hd == X_HEAD_DIM and (n_mem * nh) % 8 == 0
    g = n_mem * nh // 8
    bb = 8
    q4 = q.reshape(nb, nh, hd)
    q8 = jnp.concatenate([q4, q4], axis=1)
    kv_spec = pl.BlockSpec((None, bb, g, 8, hd), lambda i: (layer, i, 0, 0, 0))
    out = pl.pallas_call(
        functools.partial(_xattn_sample_kernel, bb=bb),
        out_shape=jax.ShapeDtypeStruct((nb, nh, hd), F32),
        grid=(nb // bb,),
        in_specs=[pl.BlockSpec((bb, 2 * nh, hd), lambda i: (i, 0, 0)), kv_spec, kv_spec],
        out_specs=pl.BlockSpec((bb, nh, hd), lambda i: (i, 0, 0)),
        compiler_params=_cparams("parallel"),
        name="xattn_sample",
    )(q8, cache_k.reshape(depth, nb, g, 8, hd), cache_v.reshape(depth, nb, g, 8, hd))
    return out.reshape(nb, nh * hd)


HG_TB = 256
HG_C = 16
HG_HEADS = 4


def _hgrn_prompt_kernel(q_ref, f_ref, i_ref, g_ref, lb_ref, og_ref, mix_ref, st_ref,
                        s_ref, qs_ref, kk_ref, bl_ref, o_ref, *, tb, c, hp):
    t = pl.program_id(2)

    @pl.when(t == 0)
    def _():
        s_ref[...] = jnp.zeros_like(s_ref)

    lb = lb_ref[...]
    ff = f_ref[...]
    lg = jnp.log(jnp.maximum(lb + (1.0 - lb) * _sigmoid(ff), F_FLOOR)) * LOG2E
    kk_ref[...] = (1.0 - lb) * _sigmoid(-ff)
    qs_ref[...] = _silu(q_ref[...])
    shift = c.bit_length() - 1
    ri = _iota2((tb, tb), 0)
    ci = _iota2((tb, tb), 1)
    same_chunk = jnp.right_shift(ri, shift) == jnp.right_shift(ci, shift)
    lmat = jnp.where(same_chunk & (ci <= ri), 1.0, 0.0).astype(BF16)
    bl_ref[...] = _exact_ldot(lmat, lg)
    sub = SUBLANES
    rowi = _iota2((sub, 1), 0)

    nch = tb // c
    heads = [slice(h * LANES, (h + 1) * LANES) for h in range(hp)]
    upd, dec = {}, {}
    for idx in range(nch):
        rows = slice(idx * c, (idx + 1) * c)
        last = slice((idx + 1) * c - 1, (idx + 1) * c)
        for h, cols in enumerate(heads):
            bend = bl_ref[last, cols]
            upd[h, idx] = _bdot_tn(i_ref[rows, cols],
                                   kk_ref[rows, cols] * jnp.exp2(bend - bl_ref[rows, cols]))
            dec[h, idx] = jnp.exp2(bend)
    states = {}
    for h in range(hp):
        states[h, 0] = s_ref[h]
        for idx in range(nch):
            states[h, idx + 1] = states[h, idx] * dec[h, idx] + upd[h, idx]
        s_ref[h] = states[h, nch]
    for idx in range(nch):
        r0 = idx * c
        rows = slice(r0, r0 + c)
        for h, cols in enumerate(heads):
            o = _bdot_nt(qs_ref[rows, cols] * jnp.exp2(bl_ref[rows, cols]), states[h, idx])
            for h0 in range(0, c, sub):
                tile = slice(r0 + h0, r0 + h0 + sub)
                bt = bl_ref[tile, cols]
                qt = qs_ref[tile, cols]
                acc = o[h0:h0 + sub, :]
                for j in range(h0 + sub):
                    src = slice(r0 + j, r0 + j + 1)
                    diff = bt - bl_ref[src, cols]
                    if j >= h0:
                        a = jnp.sum(qt * kk_ref[src, cols] * jnp.exp2(jnp.minimum(diff, 0.0)),
                                    axis=-1, keepdims=True)
                        a = jnp.where(rowi >= j - h0, a, 0.0)
                    else:
                        a = jnp.sum(qt * kk_ref[src, cols] * jnp.exp2(diff), axis=-1, keepdims=True)
                    acc = acc + a * i_ref[src, cols]
                o_ref[tile, cols] = acc
    for cols in heads:
        o = o_ref[:, cols]
        y = o * lax.rsqrt(jnp.mean(o * o, axis=-1, keepdims=True) + RMS_EPS) * og_ref[:, cols]
        mix_ref[:, cols] = (y * _silu(g_ref[:, cols])).astype(mix_ref.dtype)

    @pl.when(t == pl.num_programs(2) - 1)
    def _():
        for h in range(hp):
            st_ref[h] = s_ref[h].T


def _hgrn_prompt(p, lb, og, nb, t):
    mix = lb.shape[0]
    hd = A_HEAD_DIM
    nh = mix // hd
    hp = HG_HEADS
    assert nh % hp == 0
    ng = nh // hp
    wc = hp * hd
    tb = _tile(t, HG_TB, HG_C)
    nt = t // tb
    col = lambda off: pl.BlockSpec((tb, wc), lambda b, h, i: (b * nt + i, off + h))
    vec = pl.BlockSpec((1, wc), lambda b, h, i: (0, h))
    return pl.pallas_call(
        functools.partial(_hgrn_prompt_kernel, tb=tb, c=HG_C, hp=hp),
        out_shape=(jax.ShapeDtypeStruct((nb * t, mix), BF16),
                   jax.ShapeDtypeStruct((nb, nh, hd, hd), F32)),
        grid=(nb, ng, nt),
        in_specs=[col(0), col(ng), col(2 * ng), col(3 * ng), vec, vec],
        out_specs=(pl.BlockSpec((tb, wc), lambda b, h, i: (b * nt + i, h)),
                   pl.BlockSpec((None, hp, hd, hd), lambda b, h, i: (b, h, 0, 0))),
        scratch_shapes=[pltpu.VMEM((hp, hd, hd), F32)] + [pltpu.VMEM((tb, wc), F32)] * 4,
        compiler_params=_cparams("parallel", "parallel", "arbitrary"),
        name="hgrn_prompt",
    )(p, p, p, p, lb.reshape(1, mix), og.reshape(1, mix))


def _layer_view(so_ref, layer, first):
    if not first:
        return so_ref
    for l in range(so_ref.shape[0]):
        if l != layer:
            so_ref[l] = jnp.zeros(so_ref.shape[1:], so_ref.dtype)
    return so_ref.at[layer]


def _hgrn_sample_kernel(p_ref, lb_ref, og_ref, s_ref, *rest, bb, layer, first):
    mix_ref, so_ref = rest[-2:]
    so_ref = _layer_view(so_ref, layer, first)
    hd = s_ref.shape[-1]
    eye = (_iota2((hd, hd), 0) == _iota2((hd, hd), 1)).astype(F32)
    lb = lb_ref[...]
    col = lambda r: jnp.sum(eye[None] * r, axis=-1, keepdims=True)

    def body(b, carry):
        q, f, iv, g = p_ref[b, 0], p_ref[b, 1], p_ref[b, 2], p_ref[b, 3]
        dec = jnp.maximum(lb + (1.0 - lb) * _sigmoid(f), F_FLOOR)
        k = (1.0 - lb) * _sigmoid(-f)
        sn = s_ref[b] * col(dec) + col(k) * iv
        so_ref[b] = sn
        o = jnp.sum(sn * col(_silu(q)), axis=1, keepdims=True)
        y = o * lax.rsqrt(jnp.mean(o * o, axis=-1, keepdims=True) + RMS_EPS) * og_ref[...]
        mix_ref[b] = y * _silu(g)
        return carry

    lax.fori_loop(0, bb, body, 0)


def _stacked_state_io(states_shape, block, index_map, stacked):
    out_shape = jax.ShapeDtypeStruct(states_shape, F32)
    if stacked is None:
        all_layers = pl.BlockSpec((states_shape[0],) + tuple(block[1:]),
                                  lambda *idx: (0,) + tuple(index_map(*idx)[1:]))
        return out_shape, all_layers, [], []
    return out_shape, pl.BlockSpec(block, index_map), [pl.BlockSpec(memory_space=pl.ANY)], [stacked]


def _hgrn_sample(p, lb, og, states, layer, stacked):
    mix = lb.shape[0]
    hd = A_HEAD_DIM
    nh = mix // hd
    nb = p.shape[0]
    bb = 4
    p5 = p[:, :4 * mix].reshape(nb, 4, nh, 1, hd)
    vec = pl.BlockSpec((nh, 1, hd), lambda i: (0, 0, 0))
    st_block = (None, bb, nh, hd, hd)
    st_map = lambda i: (layer, i, 0, 0, 0)
    so_shape, so_spec, alias_specs, alias_args = _stacked_state_io(states.shape, st_block, st_map, stacked)
    out, new_states = pl.pallas_call(
        functools.partial(_hgrn_sample_kernel, bb=bb, layer=layer, first=stacked is None),
        out_shape=(jax.ShapeDtypeStruct((nb, nh, 1, hd), F32), so_shape),
        grid=(nb // bb,),
        in_specs=[pl.BlockSpec((bb, 4, nh, 1, hd), lambda i: (i, 0, 0, 0, 0)), vec, vec,
                  pl.BlockSpec(st_block, st_map)] + alias_specs,
        out_specs=(pl.BlockSpec((bb, nh, 1, hd), lambda i: (i, 0, 0, 0)), so_spec),
        input_output_aliases={4: 1} if alias_args else {},
        compiler_params=_cparams("parallel"),
        name="hgrn_sample",
    )(p5, lb.reshape(nh, 1, hd), og.reshape(nh, 1, hd), states, *alias_args)
    return out.reshape(nb, mix), new_states


RW_TR = 128
RW_TB = 256
RW_C = 64
RW_PAIRS = 6


def _softplus(x):
    return jnp.maximum(x, 0.0) + jnp.log(1.0 + jnp.exp(-jnp.abs(x)))


def _seg_sum(x, seg):
    lo = _iota2(x.shape, x.ndim - 1) < seg
    s0 = jnp.sum(jnp.where(lo, x, 0.0), axis=-1, keepdims=True)
    s1 = jnp.sum(jnp.where(lo, 0.0, x), axis=-1, keepdims=True)
    return jnp.where(lo, s0, s1)


def _rwkv_prep_kernel(p_ref, prev_ref, mu_ref, w0_ref, a0_ref, kkw_ref, ka_ref, w2_ref, a2_ref, g2_ref,
                      r_o, k_o, v_o, kk_o, kka_o, w_o, g_o, *, mix):
    p = p_ref[...]
    xs = p + (prev_ref[...] - p) * mu_ref[...]
    r = xs[:, :mix]
    k = xs[:, mix:2 * mix]
    v = xs[:, 2 * mix:3 * mix]
    la = xs[:, 3 * mix:3 * mix + LANES]
    gl = xs[:, 3 * mix + LANES:3 * mix + 3 * LANES]
    w_raw = -_softplus(-(w0_ref[...] + _bdot(jnp.tanh(la), w2_ref[...]))) - 0.5
    a = _sigmoid(a0_ref[...] + _bdot(la, a2_ref[...]))
    r_o[...] = r
    v_o[...] = v
    w_o[...] = jnp.exp(-jnp.exp(w_raw))
    g_o[...] = _bdot(_sigmoid(gl), g2_ref[...])
    k_o[...] = k * (1.0 + (a - 1.0) * ka_ref[...])
    for j in range(mix // LANES):
        sl = slice(j * LANES, (j + 1) * LANES)
        kk = k[:, sl] * kkw_ref[:, sl]
        kk = kk / jnp.maximum(jnp.sqrt(_seg_sum(kk * kk, B_HEAD_DIM)), 1e-12)
        kk_o[:, sl] = kk
        kka_o[:, sl] = kk * a[:, sl]


def _rwkv_prep(p, prev, params):
    mu, w0, a0, kkw, ka, w2p, a2p, g2p = params
    mix = w0.shape[-1]
    rows, ncol = p.shape
    tr = _tile(rows, RW_TR, 8)
    row_in = pl.BlockSpec((tr, ncol), lambda i: (i, 0))
    row_out = pl.BlockSpec((tr, mix), lambda i: (i, 0))
    consts = [mu, w0, a0, kkw, ka, w2p, a2p, g2p]
    return pl.pallas_call(
        functools.partial(_rwkv_prep_kernel, mix=mix),
        out_shape=tuple(jax.ShapeDtypeStruct((rows, mix), F32) for _ in range(7)),
        grid=(rows // tr,),
        in_specs=[row_in, row_in] + [pl.BlockSpec(x.shape, lambda i: (0, 0)) for x in consts],
        out_specs=tuple(row_out for _ in range(7)),
        compiler_params=_cparams("parallel"),
        name="rwkv_prep",
    )(p, prev, *consts)


def _rwkv_scan_kernel(pr_ref, pk_ref, pv_ref, pl_ref, mur_ref, muk_ref, muv_ref, mul_ref,
                      w0_ref, a0_ref, kkw_ref, ka_ref, w2_ref, a2_ref, g2_ref,
                      rk_ref, lg_ref, lbias_ref, mix_ref, st_ref,
                      s_ref, cr_ref, ck_ref, cv_ref, cl_ref,
                      r_ref, k_ref, v_ref, kk_ref, kka_ref, lw_ref, g_ref, *, tb, c, pp):
    n2 = 2 * c
    assert n2 == LANES
    t = pl.program_id(2)

    @pl.when(t == 0)
    def _():
        for ref in (s_ref, cr_ref, ck_ref, cv_ref, cl_ref):
            ref[...] = jnp.zeros_like(ref)

    def shifted(p_ref, carry_ref, mu_ref):
        p = p_ref[...]
        prev = jnp.where(_iota2(p.shape, 0) == 0, carry_ref[...], pltpu.roll(p, 1, axis=0))
        carry_ref[...] = p[tb - 1:tb, :]
        return p + (prev - p) * mu_ref[...]

    xl = shifted(pl_ref, cl_ref, mul_ref)
    la = xl[:, 0:LANES]
    gl = xl[:, LANES:3 * LANES]
    w_raw = -_softplus(-(w0_ref[...] + _bdot(jnp.tanh(la), w2_ref[...]))) - 0.5
    lw_ref[...] = -jnp.exp(w_raw)
    a = _sigmoid(a0_ref[...] + _bdot(la, a2_ref[...]))
    g_ref[...] = _bdot(_sigmoid(gl), g2_ref[...])
    r_ref[...] = shifted(pr_ref, cr_ref, mur_ref)
    v_ref[...] = shifted(pv_ref, cv_ref, muv_ref)
    k_raw = shifted(pk_ref, ck_ref, muk_ref)
    k_ref[...] = k_raw * (1.0 + (a - 1.0) * ka_ref[...])
    for p in range(pp):
        sl = slice(p * LANES, (p + 1) * LANES)
        kk_p = k_raw[:, sl] * kkw_ref[:, sl]
        kk_p = kk_p / jnp.maximum(jnp.sqrt(_seg_sum(kk_p * kk_p, B_HEAD_DIM)), 1e-12)
        kk_ref[:, sl] = kk_p
        kka_ref[:, sl] = kk_p * a[:, sl]

    ri = _iota2((n2, n2), 0)
    ci = _iota2((n2, n2), 1)
    blk = lambda x, n: jnp.right_shift(x, n.bit_length() - 1)
    same_head = blk(ri, c) == blk(ci, c)
    strict = same_head & (ri > ci)
    incl = same_head & (ri >= ci)
    eye = (ri == ci).astype(F32)
    tril = (_iota2((c, c), 0) >= _iota2((c, c), 1)).astype(BF16)
    head0 = _iota2((c, LANES), 1) < B_HEAD_DIM
    hd = float(B_HEAD_DIM)

    def sibling(n):
        return (blk(ri, 2 * n) == blk(ci, 2 * n)) & (blk(ri, n) > blk(ci, n))

    def stack(x):
        return jnp.concatenate([jnp.where(head0, x, 0.0), jnp.where(head0, 0.0, x)], axis=0)

    nch = tb // c
    inst = [(p, ch) for p in range(pp) for ch in range(nch)]
    each = lambda f, *lists: [f(*xs) for xs in zip(*lists)]
    win = lambda ref: [ref[ch * c:(ch + 1) * c, p * LANES:(p + 1) * LANES] for p, ch in inst]
    r, k, v, kk, kka, lw = (win(x) for x in (r_ref, k_ref, v_ref, kk_ref, kka_ref, lw_ref))
    cs = each(lambda x: _exact_ldot(tril, x), lw)
    gend = each(lambda x: jnp.exp(x[c - 1:c, :]), cs)
    ginv = each(lambda x: jnp.exp(-x), cs)
    ph = each(jnp.multiply, kka, ginv)
    kh = each(jnp.multiply, k, ginv)
    kk_m = each(lambda a, b, d: stack(a * jnp.exp(b - d)), kk, cs, lw)
    r_m = each(lambda a, b: stack(a * jnp.exp(b)), r, cs)
    v_m = each(stack, v)
    lhs = each(lambda a, b: jnp.concatenate([a, b], axis=0), kk_m, r_m)
    gp = each(lambda a, b: _bdot_nt(a, stack(b)), lhs, ph)
    gk = each(lambda a, b: _bdot_nt(a, stack(b)), lhs, kh)
    a_mat = each(lambda g: jnp.where(strict, g[0:n2], 0.0), gp)
    b_mat = each(lambda g: jnp.where(strict, g[0:n2], 0.0), gk)
    rp = each(lambda g: jnp.where(incl, g[n2:], 0.0), gp)
    rk = each(lambda g: jnp.where(incl, g[n2:], 0.0), gk)
    bv = each(_bdot, b_mat, v_m)
    rkv = each(_bdot, rk, v_m)
    t_m = each(lambda a: eye - jnp.where(blk(ri, 2) == blk(ci, 2), a, 0.0), a_mat)
    n = 2
    while n < c:
        y = each(lambda a, tm: _bdot(jnp.where(sibling(n), a, 0.0), tm), a_mat, t_m)
        t_m = each(lambda tm, yy: tm - _bdot(tm, yy), t_m, y)
        n *= 2
    ku = each(lambda tm, a, b: _bdot(tm, jnp.concatenate([a, b], axis=1)), t_m, kk_m, bv)
    rpku = each(_bdot, rp, ku)
    ml = each(lambda a, b, g: _bdot_tn(a, stack(b * g)), ku, ph, gend)
    vk = each(lambda a, b, g: _bdot_tn(a, stack(b * g)), v_m, kh, gend)
    r_t = each(lambda a, b: a - b[:, 0:n2], r_m, rpku)
    o_0 = each(lambda a, b: a - b[:, n2:], rkv, rpku)
    cst = each(lambda a, b: a - b[n2:], vk, ml)
    s = [s_ref[p] for p in range(pp)]
    o2 = [None] * len(inst)
    for ch in range(nch):
        for p in range(pp):
            i = p * nch + ch
            o2[i] = _bdot_nt(r_t[i], s[p]) + o_0[i]
            s[p] = s[p] * gend[i] - _bdot(s[p], ml[i][0:n2]) + cst[i]
    for p in range(pp):
        s_ref[p] = s[p]
    for i, (p, ch) in enumerate(inst):
        rows = slice(ch * c, (ch + 1) * c)
        cols = slice(p * LANES, (p + 1) * LANES)
        o = o2[i][0:c, :] + o2[i][c:n2, :]
        d = o - _seg_sum(o, B_HEAD_DIM) / hd
        on = (d * lax.rsqrt(_seg_sum(d * d, B_HEAD_DIM) / hd + GN_EPS) * lg_ref[:, cols]
              + lbias_ref[:, cols])
        bonus = _seg_sum(r[i] * k[i] * rk_ref[:, cols], B_HEAD_DIM) * v[i]
        mix_ref[rows, cols] = ((on + bonus) * g_ref[rows, cols]).astype(mix_ref.dtype)

    @pl.when(t == pl.num_programs(2) - 1)
    def _():
        for p in range(pp):
            s = s_ref[p]
            st_ref[2 * p] = s[0:c, 0:c]
            st_ref[2 * p + 1] = s[c:n2, c:n2]


def _rwkv_scan(p_main, p_lora, params, rk, lg, lbias, nb, t):
    mu, w0, a0, kkw, ka, w2p, a2p, g2p = params
    mix = rk.shape[-1]
    hd = B_HEAD_DIM
    nh = mix // hd
    pp = RW_PAIRS
    wc = pp * LANES
    nblk = mix // wc
    nl = p_lora.shape[1]
    assert nh % (2 * pp) == 0 and p_main.shape[1] == 3 * mix and mu.shape[1] >= 3 * mix + nl
    tb = _tile(t, RW_TB, RW_C)
    nt = t // tb
    col = lambda off: pl.BlockSpec((tb, wc), lambda b, h, i: (b * nt + i, off + h))
    vec = lambda off=0: pl.BlockSpec((1, wc), lambda b, h, i: (0, off + h))
    mat = lambda x: pl.BlockSpec((x.shape[0], wc), lambda b, h, i: (0, h))
    mu_l = lax.slice(mu, (0, 3 * mix), (1, 3 * mix + nl))
    vec1 = lambda x: x.reshape(1, mix)
    return pl.pallas_call(
        functools.partial(_rwkv_scan_kernel, tb=tb, c=RW_C, pp=pp),
        out_shape=(jax.ShapeDtypeStruct((nb * t, mix), BF16),
                   jax.ShapeDtypeStruct((nb, nh, hd, hd), F32)),
        grid=(nb, nblk, nt),
        in_specs=[col(0), col(nblk), col(2 * nblk),
                  pl.BlockSpec((tb, nl), lambda b, h, i: (b * nt + i, 0)),
                  vec(0), vec(nblk), vec(2 * nblk), pl.BlockSpec((1, nl), lambda b, h, i: (0, 0)),
                  vec(), vec(), vec(), vec(), mat(w2p), mat(a2p), mat(g2p), vec(), vec(), vec()],
        out_specs=(col(0), pl.BlockSpec((None, 2 * pp, hd, hd), lambda b, h, i: (b, h, 0, 0))),
        scratch_shapes=([pltpu.VMEM((pp, LANES, LANES), F32)] + [pltpu.VMEM((1, wc), F32)] * 3
                        + [pltpu.VMEM((1, nl), F32)] + [pltpu.VMEM((tb, wc), F32)] * 7),
        compiler_params=_cparams("parallel", "parallel", "arbitrary"),
        name="rwkv_scan",
    )(p_main, p_main, p_main, p_lora, mu, mu, mu, mu_l, w0, a0, kkw, ka, w2p, a2p, g2p,
      vec1(rk), vec1(lg), vec1(lbias))


def _rwkv_sample_kernel(r_ref, k_ref, v_ref, kk_ref, kka_ref, w_ref, g_ref, rk_ref, lg_ref, lbias_ref,
                        s_ref, *rest, layer, first):
    o_ref, so_ref, vt_ref, ot_ref = rest[-4:]
    so_ref = _layer_view(so_ref, layer, first)
    hd = s_ref.shape[1]
    r = r_ref[...]
    k = k_ref[...]
    r_t, k_t, w_t, kka_t = r.T, k.T, w_ref[...].T, kka_ref[...].T
    nkk_t = -(kk_ref[...].T)
    vt_ref[...] = v_ref[...].T
    bon_t = (r * k * rk_ref[...]).T
    for h in range(2):
        sl = slice(h * hd, (h + 1) * hd)
        nkk, kka, w, kf, rr = nkk_t[sl], kka_t[sl], w_t[sl], k_t[sl], r_t[sl]

        def body(i, carry, h=h, nkk=nkk, kka=kka, w=w, kf=kf, rr=rr):
            row = pl.ds(h * hd + i, 1)
            sv = s_ref[h, i]
            sa = jnp.sum(sv * nkk, axis=0, keepdims=True)
            sn = sv * w + sa * kka + vt_ref[row, :] * kf
            so_ref[h, i] = sn
            ot_ref[row, :] = jnp.sum(sn * rr, axis=0, keepdims=True)
            return carry

        lax.fori_loop(0, hd, body, 0, unroll=4)
    dn, bonus = [], []
    for h in range(2):
        sl = slice(h * hd, (h + 1) * hd)
        o = ot_ref[sl, :]
        d = o - jnp.mean(o, axis=0, keepdims=True)
        dn.append(d * lax.rsqrt(jnp.mean(d * d, axis=0, keepdims=True) + GN_EPS))
        bonus.append(jnp.sum(bon_t[sl], axis=0, keepdims=True) * vt_ref[sl, :])
    dn = jnp.concatenate(dn, axis=0).T
    bonus = jnp.concatenate(bonus, axis=0).T
    o_ref[...] = (dn * lg_ref[...] + lbias_ref[...] + bonus) * g_ref[...]


def _rwkv_sample(prep, rk, lg, lbias, states_t, layer, stacked):
    mix = rk.shape[-1]
    hd = B_HEAD_DIM
    nb = prep[0].shape[0]
    assert nb == LANES
    col = pl.BlockSpec((nb, LANES), lambda h: (0, h))
    vec = pl.BlockSpec((1, LANES), lambda h: (0, h))
    st_block = (None, 2, hd, hd, nb)
    st_map = lambda h: (layer, h, 0, 0, 0)
    so_shape, so_spec, alias_specs, alias_args = _stacked_state_io(states_t.shape, st_block, st_map, stacked)
    return pl.pallas_call(
        functools.partial(_rwkv_sample_kernel, layer=layer, first=stacked is None),
        out_shape=(jax.ShapeDtypeStruct((nb, mix), F32), so_shape),
        grid=(mix // LANES,),
        in_specs=[col] * 7 + [vec] * 3 + [pl.BlockSpec(st_block, st_map)] + alias_specs,
        out_specs=(col, so_spec),
        scratch_shapes=[pltpu.VMEM((LANES, nb), F32)] * 2,
        input_output_aliases={11: 1} if alias_args else {},
        compiler_params=_cparams("parallel"),
        name="rwkv_sample",
    )(*prep, rk.reshape(1, mix), lg.reshape(1, mix), lbias.reshape(1, mix), states_t, *alias_args)


def kernel(x_prompt, x_sample, cache_mem_k, cache_mem_v, state_hgrn, state_rwkv, state_rwkv_shift,
           mem_prompt, attn_norm_g, mlp_norm_g, final_norm_g, mem_norm_g, wk_mem, wv_mem,
           a_w_in, a_w_out, a_lb_logits, a_onorm_g,
           b_w_in, b_w_out, b_mu, b_w0, b_w2, b_a0, b_a2, b_g2, b_k_k, b_k_a, b_r_k, b_lnx_g, b_lnx_b,
           mlp_w1, mlp_w2):
    nb, t, d = x_prompt.shape
    ns = x_sample.shape[0]
    depth = attn_norm_g.shape[0]
    mix = a_onorm_g.shape[-1]
    xdim = d - mix
    n_mem = mem_prompt.shape[1]
    b_cols = b_mu.shape[-1]
    b_pad = -(-b_cols // LANES) * LANES
    lora0 = 3 * mix

    lbs = _lower_bounds(a_lb_logits)

    m = _rmsnorm(mem_prompt.reshape(nb * n_mem, d), mem_norm_g, BF16)
    mem_k = [_matmul(m, wk_mem, layer=l, name="mem_k") for l in range(depth)]
    mem_v = [_matmul(m, wv_mem, layer=l, name="mem_v") for l in range(depth)]
    state_rwkv_t = jnp.transpose(state_rwkv, (0, 2, 3, 4, 1))
    b_w_in_t = jnp.transpose(b_w_in, (0, 2, 1))

    rp = nb * t
    x = jnp.concatenate([x_prompt.reshape(rp, d), x_sample.reshape(ns, d)], axis=0)
    rows_of = lambda a: a[rp:rp + ns]
    hgrn_p, rwkv_p, shift_p, shift_s = [], [], [], []
    hgrn_s = rwkv_s_t = None
    for layer in range(depth):
        j = layer // 2
        g_attn = attn_norm_g[layer]
        h = _rmsnorm(x, g_attn, BF16)
        if layer % 2 == 0:
            p = _matmul(h, a_w_in, layer=j, name="a_in")
            mix_p, st_p = _hgrn_prompt(p, lbs[j], a_onorm_g[j], nb, t)
            mix_s, hgrn_s = _hgrn_sample(rows_of(p), lbs[j], a_onorm_g[j], state_hgrn, j, hgrn_s)
            hgrn_p.append(st_p)
            q_arr, q_blk = p, (4 * mix) // xdim
            qs = rows_of(p)[:, 4 * mix:]
            w_out = a_w_out
        else:
            zeros = lambda n: jnp.zeros((n, mix), F32)
            params = (
                jnp.pad(b_mu[j], (0, b_pad - b_cols)).reshape(1, b_pad),
                b_w0[j].reshape(1, mix), b_a0[j].reshape(1, mix),
                b_k_k[j].reshape(1, mix), b_k_a[j].reshape(1, mix),
                jnp.concatenate([b_w2[j], zeros(LANES - B_DECAY_LORA)], axis=0),
                jnp.concatenate([zeros(B_DECAY_LORA), b_a2[j]], axis=0),
                jnp.concatenate([b_g2[j], zeros(2 * LANES - B_GATE_LORA)], axis=0),
            )
            assert lora0 + B_DECAY_LORA + B_AAA_LORA + B_GATE_LORA == b_cols
            w_q = lax.slice(b_w_in_t, (j, b_cols, 0), (j + 1, b_w_in_t.shape[1], d)).reshape(xdim, d)
            p = _matmul(h, b_w_in_t, layer=j, n_out=lora0, wt=True, name="b_in")
            p_lora = _matmul(h, b_w_in_t, layer=j, n_out=b_pad - lora0, n_off=lora0, wt=True,
                             name="b_lora")
            q_arr, q_blk = _matmul(h, w_q, wt=True, name="b_q"), 0
            qs = rows_of(q_arr)
            ps = jnp.concatenate([rows_of(p), rows_of(p_lora)], axis=1)
            ps_prev = _matmul(state_rwkv_shift[j], b_w_in_t, layer=j, n_out=b_pad, wt=True,
                              name="b_in_shift")
            rk, lg, lbias = b_r_k[j].reshape(mix), b_lnx_g[j], b_lnx_b[j]
            mix_p, st_p = _rwkv_scan(p, p_lora, params, rk, lg, lbias, nb, t)
            mix_s, rwkv_s_t = _rwkv_sample(_rwkv_prep(ps, ps_prev, params), rk, lg, lbias,
                                           state_rwkv_t, j, rwkv_s_t)
            rwkv_p.append(st_p)
            shift_p.append(_rmsnorm(x[t - 1:rp:t], g_attn, F32))
            shift_s.append(_rmsnorm(x, g_attn, F32, rp, ns))
            w_out = b_w_out
        xo_p = _xattn_prompt(q_arr, q_blk, mem_k[layer].reshape(nb, n_mem, xdim),
                             mem_v[layer].reshape(nb, n_mem, xdim), nb, t)
        xo_s = _xattn_sample(qs, cache_mem_k, cache_mem_v, layer)
        x = _matmul(mix_p, w_out, layer=j, a2=xo_p, res=x, res_row0=0, name="out_prompt")
        x = _matmul(mix_s, w_out, layer=j, a2=xo_s, res=x, res_row0=rp, name="out_sample")
        u = _matmul(_rmsnorm(x, mlp_norm_g[layer], BF16), mlp_w1, layer=layer, act="relu2",
                    out_dtype=BF16, tn_cap=1024, name="mlp1")
        x = _matmul(u, mlp_w2, layer=layer, res=x, tm_cap=MLP2_TM, tn_cap=256, tk_cap=u.shape[1],
                    name="mlp2")

    y_p = _rmsnorm(x, final_norm_g, F32, 0, rp).reshape(nb, t, d)
    y_s = _rmsnorm(x, final_norm_g, F32, rp, ns).reshape(ns, 1, d)
    kv_shape = (depth, nb, n_mem, X_HEADS, X_HEAD_DIM)
    return (y_p, y_s, jnp.stack(mem_k).reshape(kv_shape), jnp.stack(mem_v).reshape(kv_shape),
            jnp.stack(hgrn_p), jnp.stack(rwkv_p), jnp.stack(shift_p),
            hgrn_s, jnp.transpose(rwkv_s_t, (0, 4, 1, 2, 3)), jnp.stack(shift_s))
```

```python
import functools

import jax
import jax.numpy as jnp
from jax import lax
from jax.experimental import pallas as pl
from jax.experimental.pallas import tpu as pltpu

F32 = jnp.float32
BF16 = jnp.bfloat16

RMS_EPS = 1e-6
GN_EPS = 64e-5
F_FLOOR = 1e-30
X_HEADS = 4
X_HEAD_DIM = 128
A_HEAD_DIM = 128
B_HEAD_DIM = 64
B_DECAY_LORA = 64
B_AAA_LORA = 64
B_GATE_LORA = 224

LANES = 128
SUBLANES = 8
LOG2E = 1.4426950408889634
VMEM_LIMIT = 60 * 1024 * 1024
MM_TM = 2080
MLP2_TM = 1040


def _cparams(*sem):
    return pltpu.CompilerParams(dimension_semantics=sem, vmem_limit_bytes=VMEM_LIMIT)


def _tile(n, cap, mult=8):
    if n <= cap:
        return n
    best = None
    for d in range(mult, cap + 1, mult):
        if n % d == 0:
            best = d
    assert best is not None, (n, cap, mult)
    return best


def _bdot(a, b):
    return jnp.dot(a.astype(BF16), b.astype(BF16), preferred_element_type=F32)


def _bdot_nt(a, b):
    return lax.dot_general(a.astype(BF16), b.astype(BF16), (((1,), (1,)), ((), ())),
                           preferred_element_type=F32)


def _bdot_tn(a, b):
    return lax.dot_general(a.astype(BF16), b.astype(BF16), (((0,), (0,)), ((), ())),
                           preferred_element_type=F32)


def _split3(x):
    hi = x.astype(BF16)
    r1 = x - hi.astype(F32)
    mid = r1.astype(BF16)
    lo = (r1 - mid.astype(F32)).astype(BF16)
    return hi, mid, lo


def _exact_ldot(m01, x):
    hi, mid, lo = _split3(x)
    d = lambda p: jnp.dot(m01, p, preferred_element_type=F32)
    return d(hi) + d(mid) + d(lo)


def _sigmoid(x):
    return 1.0 / (1.0 + jnp.exp(-x))


def _silu(x):
    return x * _sigmoid(x)


def _iota2(shape, axis):
    return lax.broadcasted_iota(jnp.int32, shape, axis)


def _rmsnorm_kernel(x_ref, g_ref, o_ref):
    x = x_ref[...]
    y = x * lax.rsqrt(jnp.mean(x * x, axis=-1, keepdims=True) + RMS_EPS)
    o_ref[...] = (y * g_ref[...]).astype(o_ref.dtype)


def _rmsnorm(x, g, out_dtype, row0=0, nrows=None):
    d = x.shape[1]
    m = x.shape[0] - row0 if nrows is None else nrows
    tm = _tile(m, 512, 16)
    assert row0 % tm == 0
    off = row0 // tm
    return pl.pallas_call(
        _rmsnorm_kernel,
        out_shape=jax.ShapeDtypeStruct((m, d), out_dtype),
        grid=(m // tm,),
        in_specs=[pl.BlockSpec((tm, d), lambda i: (i + off, 0)),
                  pl.BlockSpec((1, d), lambda i: (0, 0))],
        out_specs=pl.BlockSpec((tm, d), lambda i: (i, 0)),
        compiler_params=_cparams("parallel"),
        name="rmsnorm",
    )(x, g.reshape(1, d))


def _matmul_kernel(*refs, k1, nk, act, has_res, has_a2, wt):
    it = iter(refs)
    a1_ref = next(it)
    a2_ref = next(it) if has_a2 else None
    w_ref = next(it)
    res_ref = next(it) if has_res else None
    o_ref = next(it)

    if nk == 1:
        if wt:
            assert not has_a2
            acc = _bdot_nt(a1_ref[...], w_ref[...])
        else:
            acc = _bdot(a1_ref[...], w_ref[0:k1, :])
        if has_a2:
            acc = acc + _bdot(a2_ref[...], w_ref[k1:, :])
        if act == "relu2":
            acc = jnp.square(jnp.maximum(acc, 0.0))
        if has_res:
            acc = res_ref[...] + acc
        o_ref[...] = acc.astype(o_ref.dtype)
        return

    k = pl.program_id(2)
    part = _bdot(a1_ref[...], w_ref[...])

    @pl.when(k == 0)
    def _():
        o_ref[...] = (res_ref[...] + part) if has_res else part

    @pl.when(k > 0)
    def _():
        o_ref[...] += part


def _matmul(a1, w, *, layer=None, a2=None, res=None, res_row0=None, act=None, out_dtype=F32,
            n_out=None, n_off=0, wt=False, tm_cap=MM_TM, tn_cap=512, tk_cap=2048, name="matmul"):
    m, k1 = a1.shape
    k2 = a2.shape[1] if a2 is not None else 0
    kdim = k1 + k2
    n = w.shape[-2 if wt else -1] if n_out is None else n_out
    tm = _tile(m, tm_cap, 16 if a1.dtype == BF16 or out_dtype == BF16 else 8)
    tn = _tile(n, tn_cap, LANES)
    tk = _tile(kdim, tk_cap, LANES)
    nk = kdim // tk
    if nk > 1:
        assert a2 is None and act is None and out_dtype == F32 and not wt
    assert w.shape[-1 if wt else -2] == kdim and n_off % tn == 0
    j_off = n_off // tn
    w_block = (tn, tk) if wt else (tk, tn)
    w_idx = (lambda k, j: (j + j_off, k)) if wt else (lambda k, j: (k, j + j_off))
    if layer is None:
        w_spec = pl.BlockSpec(w_block, lambda i, j, k: w_idx(k, j))
    else:
        w_spec = pl.BlockSpec((None,) + w_block, lambda i, j, k: (layer,) + w_idx(k, j))
    in_specs = [pl.BlockSpec((tm, min(tk, k1)), lambda i, j, k: (i, k))]
    args = [a1]
    if a2 is not None:
        in_specs.append(pl.BlockSpec((tm, k2), lambda i, j, k: (i, 0)))
        args.append(a2)
    in_specs.append(w_spec)
    args.append(w)
    out_shape = jax.ShapeDtypeStruct((m, n), out_dtype)
    out_spec = pl.BlockSpec((tm, tn), lambda i, j, k: (i, j))
    aliases = {}
    if res is not None:
        if res_row0 is not None:
            assert res_row0 % tm == 0 and res.shape[1] == n and res.dtype == out_dtype
            i_off = res_row0 // tm
            out_spec = pl.BlockSpec((tm, tn), lambda i, j, k: (i + i_off, j))
            out_shape = jax.ShapeDtypeStruct(res.shape, out_dtype)
            aliases = {len(args): 0}
        in_specs.append(out_spec)
        args.append(res)
    kern = functools.partial(_matmul_kernel, k1=k1, nk=nk, act=act,
                             has_res=res is not None, has_a2=a2 is not None, wt=wt)
    return pl.pallas_call(
        kern,
        out_shape=out_shape,
        grid=(m // tm, n // tn, nk),
        in_specs=in_specs,
        out_specs=out_spec,
        input_output_aliases=aliases,
        compiler_params=_cparams("parallel", "parallel", "arbitrary"),
        name=name,
    )(*args)


def _lower_bounds_kernel(lg_ref, o_ref):
    lg = lg_ref[...]
    e = jnp.exp(lg - jnp.max(lg, axis=0, keepdims=True))
    p = e / jnp.sum(e, axis=0, keepdims=True)
    run = jnp.zeros_like(p[0:1])
    for l in range(lg.shape[0]):
        run = run + p[l:l + 1]
        o_ref[l:l + 1, :] = run - p[0:1]


def _lower_bounds(logits):
    return pl.pallas_call(
        _lower_bounds_kernel,
        out_shape=jax.ShapeDtypeStruct(logits.shape, F32),
        name="hgrn_lower_bounds",
    )(logits)


def _softmax_rows(s):
    e = jnp.exp(s - jnp.max(s, axis=-1, keepdims=True))
    return e / jnp.sum(e, axis=-1, keepdims=True)


def _xattn_prompt_kernel(q_ref, k_ref, v_ref, o_ref):
    scale = X_HEAD_DIM ** -0.5
    for h in range(X_HEADS):
        sl = slice(h * X_HEAD_DIM, (h + 1) * X_HEAD_DIM)
        p = _softmax_rows(_bdot_nt(q_ref[:, sl], k_ref[:, sl]) * scale)
        o_ref[:, sl] = _bdot(p, v_ref[:, sl]).astype(o_ref.dtype)


def _mem_kv_kernel(m_ref, wk_ref, wv_ref, k_ref, v_ref):
    m = m_ref[...]
    k_ref[...] = _bdot(m, wk_ref[...])
    v_ref[...] = _bdot(m, wv_ref[...])


def _mem_kv(m, wk, wv):
    depth, d, xdim = wk.shape
    rows = m.shape[0]
    w_spec = pl.BlockSpec((None, d, xdim), lambda l: (l, 0, 0))
    o_spec = pl.BlockSpec((None, rows, xdim), lambda l: (l, 0, 0))
    out = jax.ShapeDtypeStruct((depth, rows, xdim), F32)
    return pl.pallas_call(
        _mem_kv_kernel,
        out_shape=(out, out),
        grid=(depth,),
        in_specs=[pl.BlockSpec((rows, d), lambda l: (0, 0)), w_spec, w_spec],
        out_specs=(o_spec, o_spec),
        compiler_params=_cparams("parallel"),
        name="mem_kv",
    )(m, wk, wv)


def _xattn_prompt(q_arr, q_blk, mem_k, mem_v, layer, nb, t):
    xdim = mem_k.shape[-1]
    n_mem = mem_k.shape[2]
    tq = _tile(t, 512, 16)
    nt = t // tq
    kv_spec = pl.BlockSpec((None, None, n_mem, xdim), lambda b, i: (layer, b, 0, 0))
    return pl.pallas_call(
        _xattn_prompt_kernel,
        out_shape=jax.ShapeDtypeStruct((nb * t, xdim), BF16),
        grid=(nb, nt),
        in_specs=[pl.BlockSpec((tq, xdim), lambda b, i: (b * nt + i, q_blk)), kv_spec, kv_spec],
        out_specs=pl.BlockSpec((tq, xdim), lambda b, i: (b * nt + i, 0)),
        compiler_params=_cparams("parallel", "parallel"),
        name="xattn_prompt",
    )(q_arr, mem_k, mem_v)


def _xattn_sample_kernel(q_ref, k_ref, v_ref, o_ref, *, bb):
    scale = X_HEAD_DIM ** -0.5
    nh = X_HEADS

    def fold(x):
        return x + pltpu.roll(x, nh, axis=0)

    def body(b, carry):
        q8 = q_ref[b] * scale
        s = jnp.sum(k_ref[b] * q8[None], axis=-1, keepdims=True)
        mx = jnp.broadcast_to(jnp.max(s, axis=0), (2 * nh, X_HEAD_DIM))
        mx = jnp.maximum(mx, pltpu.roll(mx, nh, axis=0))
        e = jnp.exp(s - mx[None])
        num = fold(jnp.sum(e * v_ref[b], axis=0))
        o_ref[b] = (num / fold(jnp.sum(e, axis=0)))[0:nh]
        return carry

    lax.fori_loop(0, bb, body, 0, unroll=2)


def _xattn_sample(q, cache_k, cache_v, layer):
    depth, nb, n_mem, nh, hd = cache_k.shape
    assert nh == X_HEADS and hd == X_HEAD_DIM and (n_mem * nh) % 8 == 0
    g = n_mem * nh // 8
    bb = 8
    q4 = q.reshape(nb, nh, hd)
    q8 = jnp.concatenate([q4, q4], axis=1)
    kv_spec = pl.BlockSpec((None, bb, g, 8, hd), lambda i: (layer, i, 0, 0, 0))
    out = pl.pallas_call(
        functools.partial(_xattn_sample_kernel, bb=bb),
        out_shape=jax.ShapeDtypeStruct((nb, nh, hd), F32),
        grid=(nb // bb,),
        in_specs=[pl.BlockSpec((bb, 2 * nh, hd), lambda i: (i, 0, 0)), kv_spec, kv_spec],
        out_specs=pl.BlockSpec((bb, nh, hd), lambda i: (i, 0, 0)),
        compiler_params=_cparams("parallel"),
        name="xattn_sample",
    )(q8, cache_k.reshape(depth, nb, g, 8, hd), cache_v.reshape(depth, nb, g, 8, hd))
    return out.reshape(nb, nh * hd)


HG_TB = 256
HG_C = 16
HG_HEADS = 4


def _hgrn_prompt_kernel(q_ref, f_ref, i_ref, g_ref, lb_ref, og_ref, mix_ref, st_ref,
                        s_ref, qs_ref, kk_ref, bl_ref, o_ref, *, tb, c, hp):
    t = pl.program_id(2)

    @pl.when(t == 0)
    def _():
        s_ref[...] = jnp.zeros_like(s_ref)

    lb = lb_ref[...]
    ff = f_ref[...]
    lg = jnp.log(jnp.maximum(lb + (1.0 - lb) * _sigmoid(ff), F_FLOOR)) * LOG2E
    kk_ref[...] = (1.0 - lb) * _sigmoid(-ff)
    qs_ref[...] = _silu(q_ref[...])
    shift = c.bit_length() - 1
    ri = _iota2((tb, tb), 0)
    ci = _iota2((tb, tb), 1)
    same_chunk = jnp.right_shift(ri, shift) == jnp.right_shift(ci, shift)
    lmat = jnp.where(same_chunk & (ci <= ri), 1.0, 0.0).astype(BF16)
    bl_ref[...] = _exact_ldot(lmat, lg)
    sub = SUBLANES
    rowi = _iota2((sub, 1), 0)

    nch = tb // c
    heads = [slice(h * LANES, (h + 1) * LANES) for h in range(hp)]
    upd, dec = {}, {}
    for idx in range(nch):
        rows = slice(idx * c, (idx + 1) * c)
        last = slice((idx + 1) * c - 1, (idx + 1) * c)
        for h, cols in enumerate(heads):
            bend = bl_ref[last, cols]
            upd[h, idx] = _bdot_tn(i_ref[rows, cols],
                                   kk_ref[rows, cols] * jnp.exp2(bend - bl_ref[rows, cols]))
            dec[h, idx] = jnp.exp2(bend)
    states = {}
    for h in range(hp):
        states[h, 0] = s_ref[h]
        for idx in range(nch):
            states[h, idx + 1] = states[h, idx] * dec[h, idx] + upd[h, idx]
        s_ref[h] = states[h, nch]
    for idx in range(nch):
        r0 = idx * c
        rows = slice(r0, r0 + c)
        for h, cols in enumerate(heads):
            o = _bdot_nt(qs_ref[rows, cols] * jnp.exp2(bl_ref[rows, cols]), states[h, idx])
            for h0 in range(0, c, sub):
                tile = slice(r0 + h0, r0 + h0 + sub)
                bt = bl_ref[tile, cols]
                qt = qs_ref[tile, cols]
                acc = o[h0:h0 + sub, :]
                for j in range(h0 + sub):
                    src = slice(r0 + j, r0 + j + 1)
                    diff = bt - bl_ref[src, cols]
                    if j >= h0:
                        a = jnp.sum(qt * kk_ref[src, cols] * jnp.exp2(jnp.minimum(diff, 0.0)),
                                    axis=-1, keepdims=True)
                        a = jnp.where(rowi >= j - h0, a, 0.0)
                    else:
                        a = jnp.sum(qt * kk_ref[src, cols] * jnp.exp2(diff), axis=-1, keepdims=True)
                    acc = acc + a * i_ref[src, cols]
                o_ref[tile, cols] = acc
    for cols in heads:
        o = o_ref[:, cols]
        y = o * lax.rsqrt(jnp.mean(o * o, axis=-1, keepdims=True) + RMS_EPS) * og_ref[:, cols]
        mix_ref[:, cols] = (y * _silu(g_ref[:, cols])).astype(mix_ref.dtype)

    @pl.when(t == pl.num_programs(2) - 1)
    def _():
        for h in range(hp):
            st_ref[h] = s_ref[h].T


def _hgrn_prompt(p, lb, og, nb, t):
    mix = lb.shape[0]
    hd = A_HEAD_DIM
    nh = mix // hd
    hp = HG_HEADS
    assert nh % hp == 0
    ng = nh // hp
    wc = hp * hd
    tb = _tile(t, HG_TB, HG_C)
    nt = t // tb
    col = lambda off: pl.BlockSpec((tb, wc), lambda b, h, i: (b * nt + i, off + h))
    vec = pl.BlockSpec((1, wc), lambda b, h, i: (0, h))
    return pl.pallas_call(
        functools.partial(_hgrn_prompt_kernel, tb=tb, c=HG_C, hp=hp),
        out_shape=(jax.ShapeDtypeStruct((nb * t, mix), BF16),
                   jax.ShapeDtypeStruct((nb, nh, hd, hd), F32)),
        grid=(nb, ng, nt),
        in_specs=[col(0), col(ng), col(2 * ng), col(3 * ng), vec, vec],
        out_specs=(pl.BlockSpec((tb, wc), lambda b, h, i: (b * nt + i, h)),
                   pl.BlockSpec((None, hp, hd, hd), lambda b, h, i: (b, h, 0, 0))),
        scratch_shapes=[pltpu.VMEM((hp, hd, hd), F32)] + [pltpu.VMEM((tb, wc), F32)] * 4,
        compiler_params=_cparams("parallel", "parallel", "arbitrary"),
        name="hgrn_prompt",
    )(p, p, p, p, lb.reshape(1, mix), og.reshape(1, mix))


def _layer_view(so_ref, layer, first):
    if not first:
        return so_ref
    for l in range(so_ref.shape[0]):
        if l != layer:
            so_ref[l] = jnp.zeros(so_ref.shape[1:], so_ref.dtype)
    return so_ref.at[layer]


def _hgrn_sample_kernel(p_ref, lb_ref, og_ref, s_ref, *rest, bb, layer, first):
    mix_ref, so_ref = rest[-2:]
    so_ref = _layer_view(so_ref, layer, first)
    hd = s_ref.shape[-1]
    eye = (_iota2((hd, hd), 0) == _iota2((hd, hd), 1)).astype(F32)
    lb = lb_ref[...]
    col = lambda r: jnp.sum(eye[None] * r, axis=-1, keepdims=True)

    def body(b, carry):
        q, f, iv, g = p_ref[b, 0], p_ref[b, 1], p_ref[b, 2], p_ref[b, 3]
        dec = jnp.maximum(lb + (1.0 - lb) * _sigmoid(f), F_FLOOR)
        k = (1.0 - lb) * _sigmoid(-f)
        s0 = s_ref[b]
        so_ref[b] = s0 * col(dec) + col(k) * iv
        qs = _silu(q)
        qd = jnp.broadcast_to(qs * dec, (qs.shape[0], SUBLANES, hd))
        o = lax.dot_general(qd.astype(BF16), s0.astype(BF16), (((2,), (1,)), ((0,), (0,))),
                            preferred_element_type=F32)[:, 0:1, :]
        o = o + jnp.sum(qs * k, axis=-1, keepdims=True) * iv
        y = o * lax.rsqrt(jnp.mean(o * o, axis=-1, keepdims=True) + RMS_EPS) * og_ref[...]
        mix_ref[b] = y * _silu(g)
        return carry

    lax.fori_loop(0, bb, body, 0)


def _stacked_state_io(states_shape, block, index_map, stacked):
    out_shape = jax.ShapeDtypeStruct(states_shape, F32)
    if stacked is None:
        all_layers = pl.BlockSpec((states_shape[0],) + tuple(block[1:]),
                                  lambda *idx: (0,) + tuple(index_map(*idx)[1:]))
        return out_shape, all_layers, [], []
    return out_shape, pl.BlockSpec(block, index_map), [pl.BlockSpec(memory_space=pl.ANY)], [stacked]


def _hgrn_sample(p, lb, og, states, layer, stacked):
    mix = lb.shape[0]
    hd = A_HEAD_DIM
    nh = mix // hd
    nb = p.shape[0]
    bb = 4
    p5 = p[:, :4 * mix].reshape(nb, 4, nh, 1, hd)
    vec = pl.BlockSpec((nh, 1, hd), lambda i: (0, 0, 0))
    st_block = (None, bb, nh, hd, hd)
    st_map = lambda i: (layer, i, 0, 0, 0)
    so_shape, so_spec, alias_specs, alias_args = _stacked_state_io(states.shape, st_block, st_map, stacked)
    out, new_states = pl.pallas_call(
        functools.partial(_hgrn_sample_kernel, bb=bb, layer=layer, first=stacked is None),
        out_shape=(jax.ShapeDtypeStruct((nb, nh, 1, hd), F32), so_shape),
        grid=(nb // bb,),
        in_specs=[pl.BlockSpec((bb, 4, nh, 1, hd), lambda i: (i, 0, 0, 0, 0)), vec, vec,
                  pl.BlockSpec(st_block, st_map)] + alias_specs,
        out_specs=(pl.BlockSpec((bb, nh, 1, hd), lambda i: (i, 0, 0, 0)), so_spec),
        input_output_aliases={4: 1} if alias_args else {},
        compiler_params=_cparams("parallel"),
        name="hgrn_sample",
    )(p5, lb.reshape(nh, 1, hd), og.reshape(nh, 1, hd), states, *alias_args)
    return out.reshape(nb, mix), new_states


RW_TR = 128
RW_TB = 256
RW_C = 64
RW_PAIRS = 6


def _softplus(x):
    return jnp.maximum(x, 0.0) + jnp.log(1.0 + jnp.exp(-jnp.abs(x)))


def _seg_sum(x, seg):
    lo = _iota2(x.shape, x.ndim - 1) < seg
    s0 = jnp.sum(jnp.where(lo, x, 0.0), axis=-1, keepdims=True)
    s1 = jnp.sum(jnp.where(lo, 0.0, x), axis=-1, keepdims=True)
    return jnp.where(lo, s0, s1)


def _rwkv_prep_kernel(p_ref, prev_ref, mu_ref, w0_ref, a0_ref, kkw_ref, ka_ref, w2_ref, a2_ref, g2_ref,
                      r_o, k_o, v_o, kk_o, kka_o, w_o, g_o, *, mix):
    p = p_ref[...]
    xs = p + (prev_ref[...] - p) * mu_ref[...]
    r = xs[:, :mix]
    k = xs[:, mix:2 * mix]
    v = xs[:, 2 * mix:3 * mix]
    la = xs[:, 3 * mix:3 * mix + LANES]
    gl = xs[:, 3 * mix + LANES:3 * mix + 3 * LANES]
    w_raw = -_softplus(-(w0_ref[...] + _bdot(jnp.tanh(la), w2_ref[...]))) - 0.5
    a = _sigmoid(a0_ref[...] + _bdot(la, a2_ref[...]))
    r_o[...] = r
    v_o[...] = v
    w_o[...] = jnp.exp(-jnp.exp(w_raw))
    g_o[...] = _bdot(_sigmoid(gl), g2_ref[...])
    k_o[...] = k * (1.0 + (a - 1.0) * ka_ref[...])
    for j in range(mix // LANES):
        sl = slice(j * LANES, (j + 1) * LANES)
        kk = k[:, sl] * kkw_ref[:, sl]
        kk = kk / jnp.maximum(jnp.sqrt(_seg_sum(kk * kk, B_HEAD_DIM)), 1e-12)
        kk_o[:, sl] = kk
        kka_o[:, sl] = kk * a[:, sl]


def _rwkv_prep(p, prev, params):
    mu, w0, a0, kkw, ka, w2p, a2p, g2p = params
    mix = w0.shape[-1]
    rows, ncol = p.shape
    tr = _tile(rows, RW_TR, 8)
    row_in = pl.BlockSpec((tr, ncol), lambda i: (i, 0))
    row_out = pl.BlockSpec((tr, mix), lambda i: (i, 0))
    consts = [mu, w0, a0, kkw, ka, w2p, a2p, g2p]
    return pl.pallas_call(
        functools.partial(_rwkv_prep_kernel, mix=mix),
        out_shape=tuple(jax.ShapeDtypeStruct((rows, mix), F32) for _ in range(7)),
        grid=(rows // tr,),
        in_specs=[row_in, row_in] + [pl.BlockSpec(x.shape, lambda i: (0, 0)) for x in consts],
        out_specs=tuple(row_out for _ in range(7)),
        compiler_params=_cparams("parallel"),
        name="rwkv_prep",
    )(p, prev, *consts)


def _rwkv_scan_kernel(pr_ref, pk_ref, pv_ref, pl_ref, mur_ref, muk_ref, muv_ref, mul_ref,
                      w0_ref, a0_ref, kkw_ref, ka_ref, w2_ref, a2_ref, g2_ref,
                      rk_ref, lg_ref, lbias_ref, mix_ref, st_ref,
                      s_ref, cr_ref, ck_ref, cv_ref, cl_ref,
                      r_ref, k_ref, v_ref, kk_ref, kka_ref, lw_ref, g_ref, *, tb, c, pp):
    n2 = 2 * c
    assert n2 == LANES
    t = pl.program_id(2)

    @pl.when(t == 0)
    def _():
        for ref in (s_ref, cr_ref, ck_ref, cv_ref, cl_ref):
            ref[...] = jnp.zeros_like(ref)

    def shifted(p_ref, carry_ref, mu_ref):
        p = p_ref[...]
        prev = jnp.where(_iota2(p.shape, 0) == 0, carry_ref[...], pltpu.roll(p, 1, axis=0))
        carry_ref[...] = p[tb - 1:tb, :]
        return p + (prev - p) * mu_ref[...]

    xl = shifted(pl_ref, cl_ref, mul_ref)
    la = xl[:, 0:LANES]
    gl = xl[:, LANES:3 * LANES]
    w_raw = -_softplus(-(w0_ref[...] + _bdot(jnp.tanh(la), w2_ref[...]))) - 0.5
    lw_ref[...] = -jnp.exp(w_raw)
    a = _sigmoid(a0_ref[...] + _bdot(la, a2_ref[...]))
    g_ref[...] = _bdot(_sigmoid(gl), g2_ref[...])
    r_ref[...] = shifted(pr_ref, cr_ref, mur_ref)
    v_ref[...] = shifted(pv_ref, cv_ref, muv_ref)
    k_raw = shifted(pk_ref, ck_ref, muk_ref)
    k_ref[...] = k_raw * (1.0 + (a - 1.0) * ka_ref[...])
    for p in range(pp):
        sl = slice(p * LANES, (p + 1) * LANES)
        kk_p = k_raw[:, sl] * kkw_ref[:, sl]
        kk_p = kk_p / jnp.maximum(jnp.sqrt(_seg_sum(kk_p * kk_p, B_HEAD_DIM)), 1e-12)
        kk_ref[:, sl] = kk_p
        kka_ref[:, sl] = kk_p * a[:, sl]

    ri = _iota2((n2, n2), 0)
    ci = _iota2((n2, n2), 1)
    blk = lambda x, n: jnp.right_shift(x, n.bit_length() - 1)
    same_head = blk(ri, c) == blk(ci, c)
    strict = same_head & (ri > ci)
    incl = same_head & (ri >= ci)
    eye = (ri == ci).astype(F32)
    tril = (_iota2((c, c), 0) >= _iota2((c, c), 1)).astype(BF16)
    head0 = _iota2((c, LANES), 1) < B_HEAD_DIM
    hd = float(B_HEAD_DIM)

    def sibling(n):
        return (blk(ri, 2 * n) == blk(ci, 2 * n)) & (blk(ri, n) > blk(ci, n))

    def stack(x):
        return jnp.concatenate([jnp.where(head0, x, 0.0), jnp.where(head0, 0.0, x)], axis=0)

    nch = tb // c
    inst = [(p, ch) for p in range(pp) for ch in range(nch)]
    each = lambda f, *lists: [f(*xs) for xs in zip(*lists)]
    win = lambda ref: [ref[ch * c:(ch + 1) * c, p * LANES:(p + 1) * LANES] for p, ch in inst]
    r, k, v, kk, kka, lw = (win(x) for x in (r_ref, k_ref, v_ref, kk_ref, kka_ref, lw_ref))
    cs = each(lambda x: _exact_ldot(tril, x), lw)
    gend = each(lambda x: jnp.exp(x[c - 1:c, :]), cs)
    ginv = each(lambda x: jnp.exp(-x), cs)
    ph = each(jnp.multiply, kka, ginv)
    kh = each(jnp.multiply, k, ginv)
    kk_m = each(lambda a, b, d: stack(a * jnp.exp(b - d)), kk, cs, lw)
    r_m = each(lambda a, b: stack(a * jnp.exp(b)), r, cs)
    v_m = each(stack, v)
    lhs = each(lambda a, b: jnp.concatenate([a, b], axis=0), kk_m, r_m)
    gp = each(lambda a, b: _bdot_nt(a, stack(b)), lhs, ph)
    gk = each(lambda a, b: _bdot_nt(a, stack(b)), lhs, kh)
    a_mat = each(lambda g: jnp.where(strict, g[0:n2], 0.0), gp)
    b_mat = each(lambda g: jnp.where(strict, g[0:n2], 0.0), gk)
    rp = each(lambda g: jnp.where(incl, g[n2:], 0.0), gp)
    rk = each(lambda g: jnp.where(incl, g[n2:], 0.0), gk)
    bv = each(_bdot, b_mat, v_m)
    rkv = each(_bdot, rk, v_m)
    t_m = each(lambda a: eye - jnp.where(blk(ri, 2) == blk(ci, 2), a, 0.0), a_mat)
    n = 2
    while n < c:
        y = each(lambda a, tm: _bdot(jnp.where(sibling(n), a, 0.0), tm), a_mat, t_m)
        t_m = each(lambda tm, yy: tm - _bdot(tm, yy), t_m, y)
        n *= 2
    ku = each(lambda tm, a, b: _bdot(tm, jnp.concatenate([a, b], axis=1)), t_m, kk_m, bv)
    rpku = each(_bdot, rp, ku)
    ml = each(lambda a, b, g: _bdot_tn(a, stack(b * g)), ku, ph, gend)
    vk = each(lambda a, b, g: _bdot_tn(a, stack(b * g)), v_m, kh, gend)
    r_t = each(lambda a, b: a - b[:, 0:n2], r_m, rpku)
    o_0 = each(lambda a, b: a - b[:, n2:], rkv, rpku)
    cst = each(lambda a, b: a - b[n2:], vk, ml)
    s = [s_ref[p] for p in range(pp)]
    o2 = [None] * len(inst)
    for ch in range(nch):
        for p in range(pp):
            i = p * nch + ch
            o2[i] = _bdot_nt(r_t[i], s[p]) + o_0[i]
            s[p] = s[p] * gend[i] - _bdot(s[p], ml[i][0:n2]) + cst[i]
    for p in range(pp):
        s_ref[p] = s[p]
    for i, (p, ch) in enumerate(inst):
        rows = slice(ch * c, (ch + 1) * c)
        cols = slice(p * LANES, (p + 1) * LANES)
        o = o2[i][0:c, :] + o2[i][c:n2, :]
        d = o - _seg_sum(o, B_HEAD_DIM) / hd
        on = (d * lax.rsqrt(_seg_sum(d * d, B_HEAD_DIM) / hd + GN_EPS) * lg_ref[:, cols]
              + lbias_ref[:, cols])
        bonus = _seg_sum(r[i] * k[i] * rk_ref[:, cols], B_HEAD_DIM) * v[i]
        mix_ref[rows, cols] = ((on + bonus) * g_ref[rows, cols]).astype(mix_ref.dtype)

    @pl.when(t == pl.num_programs(2) - 1)
    def _():
        for p in range(pp):
            s = s_ref[p]
            st_ref[2 * p] = s[0:c, 0:c]
            st_ref[2 * p + 1] = s[c:n2, c:n2]


def _rwkv_scan(p_main, p_lora, params, rk, lg, lbias, nb, t):
    mu, w0, a0, kkw, ka, w2p, a2p, g2p = params
    mix = rk.shape[-1]
    hd = B_HEAD_DIM
    nh = mix // hd
    pp = RW_PAIRS
    wc = pp * LANES
    nblk = mix // wc
    nl = p_lora.shape[1]
    assert nh % (2 * pp) == 0 and p_main.shape[1] == 3 * mix and mu.shape[1] >= 3 * mix + nl
    tb = _tile(t, RW_TB, RW_C)
    nt = t // tb
    col = lambda off: pl.BlockSpec((tb, wc), lambda b, h, i: (b * nt + i, off + h))
    vec = lambda off=0: pl.BlockSpec((1, wc), lambda b, h, i: (0, off + h))
    mat = lambda x: pl.BlockSpec((x.shape[0], wc), lambda b, h, i: (0, h))
    mu_l = lax.slice(mu, (0, 3 * mix), (1, 3 * mix + nl))
    vec1 = lambda x: x.reshape(1, mix)
    return pl.pallas_call(
        functools.partial(_rwkv_scan_kernel, tb=tb, c=RW_C, pp=pp),
        out_shape=(jax.ShapeDtypeStruct((nb * t, mix), BF16),
                   jax.ShapeDtypeStruct((nb, nh, hd, hd), F32)),
        grid=(nb, nblk, nt),
        in_specs=[col(0), col(nblk), col(2 * nblk),
                  pl.BlockSpec((tb, nl), lambda b, h, i: (b * nt + i, 0)),
                  vec(0), vec(nblk), vec(2 * nblk), pl.BlockSpec((1, nl), lambda b, h, i: (0, 0)),
                  vec(), vec(), vec(), vec(), mat(w2p), mat(a2p), mat(g2p), vec(), vec(), vec()],
        out_specs=(col(0), pl.BlockSpec((None, 2 * pp, hd, hd), lambda b, h, i: (b, h, 0, 0))),
        scratch_shapes=([pltpu.VMEM((pp, LANES, LANES), F32)] + [pltpu.VMEM((1, wc), F32)] * 3
                        + [pltpu.VMEM((1, nl), F32)] + [pltpu.VMEM((tb, wc), F32)] * 7),
        compiler_params=_cparams("parallel", "parallel", "arbitrary"),
        name="rwkv_scan",
    )(p_main, p_main, p_main, p_lora, mu, mu, mu, mu_l, w0, a0, kkw, ka, w2p, a2p, g2p,
      vec1(rk), vec1(lg), vec1(lbias))


def _rwkv_sample_kernel(r_ref, k_ref, v_ref, kk_ref, kka_ref, w_ref, g_ref, rk_ref, lg_ref, lbias_ref,
                        s_ref, *rest, layer, first):
    o_ref, so_ref, vt_ref, ot_ref = rest[-4:]
    so_ref = _layer_view(so_ref, layer, first)
    hd = s_ref.shape[1]
    r = r_ref[...]
    k = k_ref[...]
    r_t, k_t, w_t, kka_t = r.T, k.T, w_ref[...].T, kka_ref[...].T
    nkk_t = -(kk_ref[...].T)
    vt_ref[...] = v_ref[...].T
    bon_t = (r * k * rk_ref[...]).T
    for h in range(2):
        sl = slice(h * hd, (h + 1) * hd)
        nkk, kka, w, kf, rr = nkk_t[sl], kka_t[sl], w_t[sl], k_t[sl], r_t[sl]

        def body(i, carry, h=h, nkk=nkk, kka=kka, w=w, kf=kf, rr=rr):
            row = pl.ds(h * hd + i, 1)
            sv = s_ref[h, i]
            sa = jnp.sum(sv * nkk, axis=0, keepdims=True)
            sn = sv * w + sa * kka + vt_ref[row, :] * kf
            so_ref[h, i] = sn
            ot_ref[row, :] = jnp.sum(sn * rr, axis=0, keepdims=True)
            return carry

        lax.fori_loop(0, hd, body, 0, unroll=4)
    dn, bonus = [], []
    for h in range(2):
        sl = slice(h * hd, (h + 1) * hd)
        o = ot_ref[sl, :]
        d = o - jnp.mean(o, axis=0, keepdims=True)
        dn.append(d * lax.rsqrt(jnp.mean(d * d, axis=0, keepdims=True) + GN_EPS))
        bonus.append(jnp.sum(bon_t[sl], axis=0, keepdims=True) * vt_ref[sl, :])
    dn = jnp.concatenate(dn, axis=0).T
    bonus = jnp.concatenate(bonus, axis=0).T
    o_ref[...] = (dn * lg_ref[...] + lbias_ref[...] + bonus) * g_ref[...]


def _rwkv_sample(prep, rk, lg, lbias, states_t, layer, stacked):
    mix = rk.shape[-1]
    hd = B_HEAD_DIM
    nb = prep[0].shape[0]
    assert nb == LANES
    col = pl.BlockSpec((nb, LANES), lambda h: (0, h))
    vec = pl.BlockSpec((1, LANES), lambda h: (0, h))
    st_block = (None, 2, hd, hd, nb)
    st_map = lambda h: (layer, h, 0, 0, 0)
    so_shape, so_spec, alias_specs, alias_args = _stacked_state_io(states_t.shape, st_block, st_map, stacked)
    return pl.pallas_call(
        functools.partial(_rwkv_sample_kernel, layer=layer, first=stacked is None),
        out_shape=(jax.ShapeDtypeStruct((nb, mix), F32), so_shape),
        grid=(mix // LANES,),
        in_specs=[col] * 7 + [vec] * 3 + [pl.BlockSpec(st_block, st_map)] + alias_specs,
        out_specs=(col, so_spec),
        scratch_shapes=[pltpu.VMEM((LANES, nb), F32)] * 2,
        input_output_aliases={11: 1} if alias_args else {},
        compiler_params=_cparams("parallel"),
        name="rwkv_sample",
    )(*prep, rk.reshape(1, mix), lg.reshape(1, mix), lbias.reshape(1, mix), states_t, *alias_args)


def kernel(x_prompt, x_sample, cache_mem_k, cache_mem_v, state_hgrn, state_rwkv, state_rwkv_shift,
           mem_prompt, attn_norm_g, mlp_norm_g, final_norm_g, mem_norm_g, wk_mem, wv_mem,
           a_w_in, a_w_out, a_lb_logits, a_onorm_g,
           b_w_in, b_w_out, b_mu, b_w0, b_w2, b_a0, b_a2, b_g2, b_k_k, b_k_a, b_r_k, b_lnx_g, b_lnx_b,
           mlp_w1, mlp_w2):
    nb, t, d = x_prompt.shape
    ns = x_sample.shape[0]
    depth = attn_norm_g.shape[0]
    mix = a_onorm_g.shape[-1]
    xdim = d - mix
    n_mem = mem_prompt.shape[1]
    b_cols = b_mu.shape[-1]
    b_pad = -(-b_cols // LANES) * LANES
    lora0 = 3 * mix

    lbs = _lower_bounds(a_lb_logits)

    m = _rmsnorm(mem_prompt.reshape(nb * n_mem, d), mem_norm_g, BF16)
    mem_k, mem_v = (a.reshape(depth, nb, n_mem, xdim) for a in _mem_kv(m, wk_mem, wv_mem))
    state_rwkv_t = jnp.transpose(state_rwkv, (0, 2, 3, 4, 1))
    b_w_in_t = jnp.transpose(b_w_in, (0, 2, 1))

    rp = nb * t
    x = jnp.concatenate([x_prompt.reshape(rp, d), x_sample.reshape(ns, d)], axis=0)
    rows_of = lambda a: a[rp:rp + ns]
    hgrn_p, rwkv_p, shift_p, shift_s = [], [], [], []
    hgrn_s = rwkv_s_t = None
    for layer in range(depth):
        j = layer // 2
        g_attn = attn_norm_g[layer]
        h = _rmsnorm(x, g_attn, BF16)
        if layer % 2 == 0:
            p = _matmul(h, a_w_in, layer=j, name="a_in")
            mix_p, st_p = _hgrn_prompt(p, lbs[j], a_onorm_g[j], nb, t)
            mix_s, hgrn_s = _hgrn_sample(rows_of(p), lbs[j], a_onorm_g[j], state_hgrn, j, hgrn_s)
            hgrn_p.append(st_p)
            q_arr, q_blk = p, (4 * mix) // xdim
            qs = rows_of(p)[:, 4 * mix:]
            w_out = a_w_out
        else:
            zeros = lambda n: jnp.zeros((n, mix), F32)
            params = (
                jnp.pad(b_mu[j], (0, b_pad - b_cols)).reshape(1, b_pad),
                b_w0[j].reshape(1, mix), b_a0[j].reshape(1, mix),
                b_k_k[j].reshape(1, mix), b_k_a[j].reshape(1, mix),
                jnp.concatenate([b_w2[j], zeros(LANES - B_DECAY_LORA)], axis=0),
                jnp.concatenate([zeros(B_DECAY_LORA), b_a2[j]], axis=0),
                jnp.concatenate([b_g2[j], zeros(2 * LANES - B_GATE_LORA)], axis=0),
            )
            assert lora0 + B_DECAY_LORA + B_AAA_LORA + B_GATE_LORA == b_cols
            w_q = lax.slice(b_w_in_t, (j, b_cols, 0), (j + 1, b_w_in_t.shape[1], d)).reshape(xdim, d)
            p = _matmul(h, b_w_in_t, layer=j, n_out=lora0, wt=True, name="b_in")
            p_lora = _matmul(h, b_w_in_t, layer=j, n_out=b_pad - lora0, n_off=lora0, wt=True,
                             name="b_lora")
            q_arr, q_blk = _matmul(h, w_q, wt=True, name="b_q"), 0
            qs = rows_of(q_arr)
            ps = jnp.concatenate([rows_of(p), rows_of(p_lora)], axis=1)
            ps_prev = _matmul(state_rwkv_shift[j], b_w_in_t, layer=j, n_out=b_pad, wt=True,
                              name="b_in_shift")
            rk, lg, lbias = b_r_k[j].reshape(mix), b_lnx_g[j], b_lnx_b[j]
            mix_p, st_p = _rwkv_scan(p, p_lora, params, rk, lg, lbias, nb, t)
            mix_s, rwkv_s_t = _rwkv_sample(_rwkv_prep(ps, ps_prev, params), rk, lg, lbias,
                                           state_rwkv_t, j, rwkv_s_t)
            rwkv_p.append(st_p)
            shift_p.append(_rmsnorm(x[t - 1:rp:t], g_attn, F32))
            shift_s.append(_rmsnorm(x, g_attn, F32, rp, ns))
            w_out = b_w_out
        xo_p = _xattn_prompt(q_arr, q_blk, mem_k, mem_v, layer, nb, t)
        xo_s = _xattn_sample(qs, cache_mem_k, cache_mem_v, layer)
        x = _matmul(mix_p, w_out, layer=j, a2=xo_p, res=x, res_row0=0, name="out_prompt")
        x = _matmul(mix_s, w_out, layer=j, a2=xo_s, res=x, res_row0=rp, name="out_sample")
        u = _matmul(_rmsnorm(x, mlp_norm_g[layer], BF16), mlp_w1, layer=layer, act="relu2",
                    out_dtype=BF16, tn_cap=1024, name="mlp1")
        x = _matmul(u, mlp_w2, layer=layer, res=x, tm_cap=MLP2_TM, tn_cap=256, tk_cap=u.shape[1],
                    name="mlp2")

    y_p = _rmsnorm(x, final_norm_g, F32, 0, rp).reshape(nb, t, d)
    y_s = _rmsnorm(x, final_norm_g, F32, rp, ns).reshape(ns, 1, d)
    kv_shape = (depth, nb, n_mem, X_HEADS, X_HEAD_DIM)
    return (y_p, y_s, mem_k.reshape(kv_shape), mem_v.reshape(kv_shape),
            jnp.stack(hgrn_p), jnp.stack(rwkv_p), jnp.stack(shift_p),
            hgrn_s, jnp.transpose(rwkv_s_t, (0, 4, 1, 2, 3)), jnp.stack(shift_s))
```

```python
import functools

import jax
import jax.numpy as jnp
from jax import lax
from jax.experimental import pallas as pl
from jax.experimental.pallas import tpu as pltpu

F32 = jnp.float32
BF16 = jnp.bfloat16

RMS_EPS = 1e-6
GN_EPS = 64e-5
F_FLOOR = 1e-30
X_HEADS = 4
X_HEAD_DIM = 128
A_HEAD_DIM = 128
B_HEAD_DIM = 64
B_DECAY_LORA = 64
B_AAA_LORA = 64
B_GATE_LORA = 224

LANES = 128
SUBLANES = 8
LOG2E = 1.4426950408889634
VMEM_LIMIT = 60 * 1024 * 1024
MM_TM = 2080
MLP2_TM = 1040
NORM_TM = 1040


def _cparams(*sem):
    return pltpu.CompilerParams(dimension_semantics=sem, vmem_limit_bytes=VMEM_LIMIT)


def _tile(n, cap, mult=8):
    if n <= cap:
        return n
    best = None
    for d in range(mult, cap + 1, mult):
        if n % d == 0:
            best = d
    assert best is not None, (n, cap, mult)
    return best


def _bdot(a, b):
    return jnp.dot(a.astype(BF16), b.astype(BF16), preferred_element_type=F32)


def _bdot_nt(a, b):
    return lax.dot_general(a.astype(BF16), b.astype(BF16), (((1,), (1,)), ((), ())),
                           preferred_element_type=F32)


def _bdot_tn(a, b):
    return lax.dot_general(a.astype(BF16), b.astype(BF16), (((0,), (0,)), ((), ())),
                           preferred_element_type=F32)


def _split3(x):
    hi = x.astype(BF16)
    r1 = x - hi.astype(F32)
    mid = r1.astype(BF16)
    lo = (r1 - mid.astype(F32)).astype(BF16)
    return hi, mid, lo


def _exact_ldot(m01, x):
    hi, mid, lo = _split3(x)
    d = lambda p: jnp.dot(m01, p, preferred_element_type=F32)
    return d(hi) + d(mid) + d(lo)


def _sigmoid(x):
    return 1.0 / (1.0 + jnp.exp(-x))


def _silu(x):
    return x * _sigmoid(x)


def _iota2(shape, axis):
    return lax.broadcasted_iota(jnp.int32, shape, axis)


def _rmsnorm_kernel(x_ref, g_ref, o_ref):
    x = x_ref[...]
    y = x * lax.rsqrt(jnp.mean(x * x, axis=-1, keepdims=True) + RMS_EPS)
    o_ref[...] = (y * g_ref[...]).astype(o_ref.dtype)


def _rmsnorm(x, g, out_dtype, row0=0, nrows=None):
    d = x.shape[1]
    m = x.shape[0] - row0 if nrows is None else nrows
    tm = _tile(m, NORM_TM, 16)
    assert row0 % tm == 0
    off = row0 // tm
    return pl.pallas_call(
        _rmsnorm_kernel,
        out_shape=jax.ShapeDtypeStruct((m, d), out_dtype),
        grid=(m // tm,),
        in_specs=[pl.BlockSpec((tm, d), lambda i: (i + off, 0)),
                  pl.BlockSpec((1, d), lambda i: (0, 0))],
        out_specs=pl.BlockSpec((tm, d), lambda i: (i, 0)),
        compiler_params=_cparams("parallel"),
        name="rmsnorm",
    )(x, g.reshape(1, d))


def _matmul_kernel(*refs, k1, nk, act, has_res, has_a2, wt):
    it = iter(refs)
    a1_ref = next(it)
    a2_ref = next(it) if has_a2 else None
    w_ref = next(it)
    res_ref = next(it) if has_res else None
    o_ref = next(it)

    if nk == 1:
        if wt:
            assert not has_a2
            acc = _bdot_nt(a1_ref[...], w_ref[...])
        else:
            acc = _bdot(a1_ref[...], w_ref[0:k1, :])
        if has_a2:
            acc = acc + _bdot(a2_ref[...], w_ref[k1:, :])
        if act == "relu2":
            acc = jnp.square(jnp.maximum(acc, 0.0))
        if has_res:
            acc = res_ref[...] + acc
        o_ref[...] = acc.astype(o_ref.dtype)
        return

    k = pl.program_id(2)
    part = _bdot(a1_ref[...], w_ref[...])

    @pl.when(k == 0)
    def _():
        o_ref[...] = (res_ref[...] + part) if has_res else part

    @pl.when(k > 0)
    def _():
        o_ref[...] += part


def _matmul(a1, w, *, layer=None, a2=None, res=None, res_row0=None, act=None, out_dtype=F32,
            n_out=None, n_off=0, wt=False, tm_cap=MM_TM, tn_cap=512, tk_cap=2048, name="matmul"):
    m, k1 = a1.shape
    k2 = a2.shape[1] if a2 is not None else 0
    kdim = k1 + k2
    n = w.shape[-2 if wt else -1] if n_out is None else n_out
    tm = _tile(m, tm_cap, 16 if a1.dtype == BF16 or out_dtype == BF16 else 8)
    tn = _tile(n, tn_cap, LANES)
    tk = _tile(kdim, tk_cap, LANES)
    nk = kdim // tk
    if nk > 1:
        assert a2 is None and act is None and out_dtype == F32 and not wt
    assert w.shape[-1 if wt else -2] == kdim and n_off % tn == 0
    j_off = n_off // tn
    w_block = (tn, tk) if wt else (tk, tn)
    w_idx = (lambda k, j: (j + j_off, k)) if wt else (lambda k, j: (k, j + j_off))
    if layer is None:
        w_spec = pl.BlockSpec(w_block, lambda i, j, k: w_idx(k, j))
    else:
        w_spec = pl.BlockSpec((None,) + w_block, lambda i, j, k: (layer,) + w_idx(k, j))
    in_specs = [pl.BlockSpec((tm, min(tk, k1)), lambda i, j, k: (i, k))]
    args = [a1]
    if a2 is not None:
        in_specs.append(pl.BlockSpec((tm, k2), lambda i, j, k: (i, 0)))
        args.append(a2)
    in_specs.append(w_spec)
    args.append(w)
    out_shape = jax.ShapeDtypeStruct((m, n), out_dtype)
    out_spec = pl.BlockSpec((tm, tn), lambda i, j, k: (i, j))
    aliases = {}
    if res is not None:
        if res_row0 is not None:
            assert res_row0 % tm == 0 and res.shape[1] == n and res.dtype == out_dtype
            i_off = res_row0 // tm
            out_spec = pl.BlockSpec((tm, tn), lambda i, j, k: (i + i_off, j))
            out_shape = jax.ShapeDtypeStruct(res.shape, out_dtype)
            aliases = {len(args): 0}
        in_specs.append(out_spec)
        args.append(res)
    kern = functools.partial(_matmul_kernel, k1=k1, nk=nk, act=act,
                             has_res=res is not None, has_a2=a2 is not None, wt=wt)
    return pl.pallas_call(
        kern,
        out_shape=out_shape,
        grid=(m // tm, n // tn, nk),
        in_specs=in_specs,
        out_specs=out_spec,
        input_output_aliases=aliases,
        compiler_params=_cparams("parallel", "parallel", "arbitrary"),
        name=name,
    )(*args)


def _lower_bounds_kernel(lg_ref, o_ref):
    lg = lg_ref[...]
    e = jnp.exp(lg - jnp.max(lg, axis=0, keepdims=True))
    p = e / jnp.sum(e, axis=0, keepdims=True)
    run = jnp.zeros_like(p[0:1])
    for l in range(lg.shape[0]):
        run = run + p[l:l + 1]
        o_ref[l:l + 1, :] = run - p[0:1]


def _lower_bounds(logits):
    return pl.pallas_call(
        _lower_bounds_kernel,
        out_shape=jax.ShapeDtypeStruct(logits.shape, F32),
        name="hgrn_lower_bounds",
    )(logits)


def _softmax_rows(s):
    e = jnp.exp(s - jnp.max(s, axis=-1, keepdims=True))
    return e / jnp.sum(e, axis=-1, keepdims=True)


def _xattn_prompt_kernel(q_ref, k_ref, v_ref, o_ref):
    scale = X_HEAD_DIM ** -0.5
    for h in range(X_HEADS):
        sl = slice(h * X_HEAD_DIM, (h + 1) * X_HEAD_DIM)
        p = _softmax_rows(_bdot_nt(q_ref[:, sl], k_ref[:, sl]) * scale)
        o_ref[:, sl] = _bdot(p, v_ref[:, sl]).astype(o_ref.dtype)


def _mem_kv_kernel(m_ref, wk_ref, wv_ref, k_ref, v_ref):
    m = m_ref[...]
    k_ref[...] = _bdot(m, wk_ref[...])
    v_ref[...] = _bdot(m, wv_ref[...])


def _mem_kv(m, wk, wv):
    depth, d, xdim = wk.shape
    rows = m.shape[0]
    w_spec = pl.BlockSpec((None, d, xdim), lambda l: (l, 0, 0))
    o_spec = pl.BlockSpec((None, rows, xdim), lambda l: (l, 0, 0))
    out = jax.ShapeDtypeStruct((depth, rows, xdim), F32)
    return pl.pallas_call(
        _mem_kv_kernel,
        out_shape=(out, out),
        grid=(depth,),
        in_specs=[pl.BlockSpec((rows, d), lambda l: (0, 0)), w_spec, w_spec],
        out_specs=(o_spec, o_spec),
        compiler_params=_cparams("parallel"),
        name="mem_kv",
    )(m, wk, wv)


def _xattn_prompt(q_arr, q_blk, mem_k, mem_v, layer, nb, t):
    xdim = mem_k.shape[-1]
    n_mem = mem_k.shape[2]
    tq = _tile(t, 512, 16)
    nt = t // tq
    kv_spec = pl.BlockSpec((None, None, n_mem, xdim), lambda b, i: (layer, b, 0, 0))
    return pl.pallas_call(
        _xattn_prompt_kernel,
        out_shape=jax.ShapeDtypeStruct((nb * t, xdim), BF16),
        grid=(nb, nt),
        in_specs=[pl.BlockSpec((tq, xdim), lambda b, i: (b * nt + i, q_blk)), kv_spec, kv_spec],
        out_specs=pl.BlockSpec((tq, xdim), lambda b, i: (b * nt + i, 0)),
        compiler_params=_cparams("parallel", "parallel"),
        name="xattn_prompt",
    )(q_arr, mem_k, mem_v)


def _xattn_sample_kernel(q_ref, k_ref, v_ref, o_ref, *, bb):
    scale = X_HEAD_DIM ** -0.5
    nh = X_HEADS

    def fold(x):
        return x + pltpu.roll(x, nh, axis=0)

    def body(b, carry):
        q8 = q_ref[b] * scale
        s = jnp.sum(k_ref[b] * q8[None], axis=-1, keepdims=True)
        mx = jnp.broadcast_to(jnp.max(s, axis=0), (2 * nh, X_HEAD_DIM))
        mx = jnp.maximum(mx, pltpu.roll(mx, nh, axis=0))
        e = jnp.exp(s - mx[None])
        num = fold(jnp.sum(e * v_ref[b], axis=0))
        o_ref[b] = (num / fold(jnp.sum(e, axis=0)))[0:nh]
        return carry

    lax.fori_loop(0, bb, body, 0, unroll=4)


def _xattn_sample(q, cache_k, cache_v, layer):
    depth, nb, n_mem, nh, hd = cache_k.shape
    assert nh == X_HEADS and hd == X_HEAD_DIM and (n_mem * nh) % 8 == 0
    g = n_mem * nh // 8
    bb = 8
    q4 = q.reshape(nb, nh, hd)
    q8 = jnp.concatenate([q4, q4], axis=1)
    kv_spec = pl.BlockSpec((None, bb, g, 8, hd), lambda i: (layer, i, 0, 0, 0))
    out = pl.pallas_call(
        functools.partial(_xattn_sample_kernel, bb=bb),
        out_shape=jax.ShapeDtypeStruct((nb, nh, hd), F32),
        grid=(nb // bb,),
        in_specs=[pl.BlockSpec((bb, 2 * nh, hd), lambda i: (i, 0, 0)), kv_spec, kv_spec],
        out_specs=pl.BlockSpec((bb, nh, hd), lambda i: (i, 0, 0)),
        compiler_params=_cparams("parallel"),
        name="xattn_sample",
    )(q8, cache_k.reshape(depth, nb, g, 8, hd), cache_v.reshape(depth, nb, g, 8, hd))
    return out.reshape(nb, nh * hd)


HG_TB = 256
HG_C = 16
HG_HEADS = 4


def _hgrn_prompt_kernel(q_ref, f_ref, i_ref, g_ref, lb_ref, og_ref, mix_ref, st_ref,
                        s_ref, qs_ref, kk_ref, bl_ref, o_ref, *, tb, c, hp):
    t = pl.program_id(2)

    @pl.when(t == 0)
    def _():
        s_ref[...] = jnp.zeros_like(s_ref)

    lb = lb_ref[...]
    ff = f_ref[...]
    lg = jnp.log(jnp.maximum(lb + (1.0 - lb) * _sigmoid(ff), F_FLOOR)) * LOG2E
    kk_ref[...] = (1.0 - lb) * _sigmoid(-ff)
    qs_ref[...] = _silu(q_ref[...])
    shift = c.bit_length() - 1
    ri = _iota2((tb, tb), 0)
    ci = _iota2((tb, tb), 1)
    same_chunk = jnp.right_shift(ri, shift) == jnp.right_shift(ci, shift)
    lmat = jnp.where(same_chunk & (ci <= ri), 1.0, 0.0).astype(BF16)
    bl_ref[...] = _exact_ldot(lmat, lg)
    sub = SUBLANES
    rowi = _iota2((sub, 1), 0)

    nch = tb // c
    heads = [slice(h * LANES, (h + 1) * LANES) for h in range(hp)]
    upd, dec = {}, {}
    for idx in range(nch):
        rows = slice(idx * c, (idx + 1) * c)
        last = slice((idx + 1) * c - 1, (idx + 1) * c)
        for h, cols in enumerate(heads):
            bend = bl_ref[last, cols]
            upd[h, idx] = _bdot_tn(i_ref[rows, cols],
                                   kk_ref[rows, cols] * jnp.exp2(bend - bl_ref[rows, cols]))
            dec[h, idx] = jnp.exp2(bend)
    states = {}
    for h in range(hp):
        states[h, 0] = s_ref[h]
        for idx in range(nch):
            states[h, idx + 1] = states[h, idx] * dec[h, idx] + upd[h, idx]
        s_ref[h] = states[h, nch]
    for idx in range(nch):
        r0 = idx * c
        rows = slice(r0, r0 + c)
        for h, cols in enumerate(heads):
            o = _bdot_nt(qs_ref[rows, cols] * jnp.exp2(bl_ref[rows, cols]), states[h, idx])
            for h0 in range(0, c, sub):
                tile = slice(r0 + h0, r0 + h0 + sub)
                bt = bl_ref[tile, cols]
                qt = qs_ref[tile, cols]
                acc = o[h0:h0 + sub, :]
                for j in range(h0 + sub):
                    src = slice(r0 + j, r0 + j + 1)
                    diff = bt - bl_ref[src, cols]
                    if j >= h0:
                        a = jnp.sum(qt * kk_ref[src, cols] * jnp.exp2(jnp.minimum(diff, 0.0)),
                                    axis=-1, keepdims=True)
                        a = jnp.where(rowi >= j - h0, a, 0.0)
                    else:
                        a = jnp.sum(qt * kk_ref[src, cols] * jnp.exp2(diff), axis=-1, keepdims=True)
                    acc = acc + a * i_ref[src, cols]
                o_ref[tile, cols] = acc
    for cols in heads:
        o = o_ref[:, cols]
        y = o * lax.rsqrt(jnp.mean(o * o, axis=-1, keepdims=True) + RMS_EPS) * og_ref[:, cols]
        mix_ref[:, cols] = (y * _silu(g_ref[:, cols])).astype(mix_ref.dtype)

    @pl.when(t == pl.num_programs(2) - 1)
    def _():
        for h in range(hp):
            st_ref[h] = s_ref[h].T


def _hgrn_prompt(p, lb, og, nb, t):
    mix = lb.shape[0]
    hd = A_HEAD_DIM
    nh = mix // hd
    hp = HG_HEADS
    assert nh % hp == 0
    ng = nh // hp
    wc = hp * hd
    tb = _tile(t, HG_TB, HG_C)
    nt = t // tb
    col = lambda off: pl.BlockSpec((tb, wc), lambda b, h, i: (b * nt + i, off + h))
    vec = pl.BlockSpec((1, wc), lambda b, h, i: (0, h))
    return pl.pallas_call(
        functools.partial(_hgrn_prompt_kernel, tb=tb, c=HG_C, hp=hp),
        out_shape=(jax.ShapeDtypeStruct((nb * t, mix), BF16),
                   jax.ShapeDtypeStruct((nb, nh, hd, hd), F32)),
        grid=(nb, ng, nt),
        in_specs=[col(0), col(ng), col(2 * ng), col(3 * ng), vec, vec],
        out_specs=(pl.BlockSpec((tb, wc), lambda b, h, i: (b * nt + i, h)),
                   pl.BlockSpec((None, hp, hd, hd), lambda b, h, i: (b, h, 0, 0))),
        scratch_shapes=[pltpu.VMEM((hp, hd, hd), F32)] + [pltpu.VMEM((tb, wc), F32)] * 4,
        compiler_params=_cparams("parallel", "parallel", "arbitrary"),
        name="hgrn_prompt",
    )(p, p, p, p, lb.reshape(1, mix), og.reshape(1, mix))


def _layer_view(so_ref, layer, first):
    if not first:
        return so_ref
    for l in range(so_ref.shape[0]):
        if l != layer:
            so_ref[l] = jnp.zeros(so_ref.shape[1:], so_ref.dtype)
    return so_ref.at[layer]


def _hgrn_sample_kernel(p_ref, lb_ref, og_ref, s_ref, *rest, bb, layer, first):
    mix_ref, so_ref = rest[-2:]
    so_ref = _layer_view(so_ref, layer, first)
    hd = s_ref.shape[-1]
    eye = (_iota2((hd, hd), 0) == _iota2((hd, hd), 1)).astype(F32)
    lb = lb_ref[...]
    col = lambda r: jnp.sum(eye[None] * r, axis=-1, keepdims=True)

    def body(b, carry):
        q, f, iv, g = p_ref[b, 0], p_ref[b, 1], p_ref[b, 2], p_ref[b, 3]
        dec = jnp.maximum(lb + (1.0 - lb) * _sigmoid(f), F_FLOOR)
        k = (1.0 - lb) * _sigmoid(-f)
        s0 = s_ref[b]
        so_ref[b] = s0 * col(dec) + col(k) * iv
        qs = _silu(q)
        qd = jnp.broadcast_to(qs * dec, (qs.shape[0], SUBLANES, hd))
        o = lax.dot_general(qd.astype(BF16), s0.astype(BF16), (((2,), (1,)), ((0,), (0,))),
                            preferred_element_type=F32)[:, 0:1, :]
        o = o + jnp.sum(qs * k, axis=-1, keepdims=True) * iv
        y = o * lax.rsqrt(jnp.mean(o * o, axis=-1, keepdims=True) + RMS_EPS) * og_ref[...]
        mix_ref[b] = y * _silu(g)
        return carry

    lax.fori_loop(0, bb, body, 0)


def _stacked_state_io(states_shape, block, index_map, stacked):
    out_shape = jax.ShapeDtypeStruct(states_shape, F32)
    if stacked is None:
        all_layers = pl.BlockSpec((states_shape[0],) + tuple(block[1:]),
                                  lambda *idx: (0,) + tuple(index_map(*idx)[1:]))
        return out_shape, all_layers, [], []
    return out_shape, pl.BlockSpec(block, index_map), [pl.BlockSpec(memory_space=pl.ANY)], [stacked]


def _hgrn_sample(p, lb, og, states, layer, stacked):
    mix = lb.shape[0]
    hd = A_HEAD_DIM
    nh = mix // hd
    nb = p.shape[0]
    bb = 4
    p5 = p[:, :4 * mix].reshape(nb, 4, nh, 1, hd)
    vec = pl.BlockSpec((nh, 1, hd), lambda i: (0, 0, 0))
    st_block = (None, bb, nh, hd, hd)
    st_map = lambda i: (layer, i, 0, 0, 0)
    so_shape, so_spec, alias_specs, alias_args = _stacked_state_io(states.shape, st_block, st_map, stacked)
    out, new_states = pl.pallas_call(
        functools.partial(_hgrn_sample_kernel, bb=bb, layer=layer, first=stacked is None),
        out_shape=(jax.ShapeDtypeStruct((nb, nh, 1, hd), F32), so_shape),
        grid=(nb // bb,),
        in_specs=[pl.BlockSpec((bb, 4, nh, 1, hd), lambda i: (i, 0, 0, 0, 0)), vec, vec,
                  pl.BlockSpec(st_block, st_map)] + alias_specs,
        out_specs=(pl.BlockSpec((bb, nh, 1, hd), lambda i: (i, 0, 0, 0)), so_spec),
        input_output_aliases={4: 1} if alias_args else {},
        compiler_params=_cparams("parallel"),
        name="hgrn_sample",
    )(p5, lb.reshape(nh, 1, hd), og.reshape(nh, 1, hd), states, *alias_args)
    return out.reshape(nb, mix), new_states


RW_TR = 128
RW_TB = 256
RW_C = 64
RW_PAIRS = 6


def _softplus(x):
    return jnp.maximum(x, 0.0) + jnp.log(1.0 + jnp.exp(-jnp.abs(x)))


def _seg_sum(x, seg):
    lo = _iota2(x.shape, x.ndim - 1) < seg
    s0 = jnp.sum(jnp.where(lo, x, 0.0), axis=-1, keepdims=True)
    s1 = jnp.sum(jnp.where(lo, 0.0, x), axis=-1, keepdims=True)
    return jnp.where(lo, s0, s1)


def _rwkv_prep_kernel(p_ref, prev_ref, mu_ref, w0_ref, a0_ref, kkw_ref, ka_ref, w2_ref, a2_ref, g2_ref,
                      r_o, k_o, v_o, kk_o, kka_o, w_o, g_o, *, mix):
    p = p_ref[...]
    xs = p + (prev_ref[...] - p) * mu_ref[...]
    r = xs[:, :mix]
    k = xs[:, mix:2 * mix]
    v = xs[:, 2 * mix:3 * mix]
    la = xs[:, 3 * mix:3 * mix + LANES]
    gl = xs[:, 3 * mix + LANES:3 * mix + 3 * LANES]
    w_raw = -_softplus(-(w0_ref[...] + _bdot(jnp.tanh(la), w2_ref[...]))) - 0.5
    a = _sigmoid(a0_ref[...] + _bdot(la, a2_ref[...]))
    r_o[...] = r
    v_o[...] = v
    w_o[...] = jnp.exp(-jnp.exp(w_raw))
    g_o[...] = _bdot(_sigmoid(gl), g2_ref[...])
    k_o[...] = k * (1.0 + (a - 1.0) * ka_ref[...])
    for j in range(mix // LANES):
        sl = slice(j * LANES, (j + 1) * LANES)
        kk = k[:, sl] * kkw_ref[:, sl]
        kk = kk / jnp.maximum(jnp.sqrt(_seg_sum(kk * kk, B_HEAD_DIM)), 1e-12)
        kk_o[:, sl] = kk
        kka_o[:, sl] = kk * a[:, sl]


def _rwkv_prep(p, prev, params):
    mu, w0, a0, kkw, ka, w2p, a2p, g2p = params
    mix = w0.shape[-1]
    rows, ncol = p.shape
    tr = _tile(rows, RW_TR, 8)
    row_in = pl.BlockSpec((tr, ncol), lambda i: (i, 0))
    row_out = pl.BlockSpec((tr, mix), lambda i: (i, 0))
    consts = [mu, w0, a0, kkw, ka, w2p, a2p, g2p]
    return pl.pallas_call(
        functools.partial(_rwkv_prep_kernel, mix=mix),
        out_shape=tuple(jax.ShapeDtypeStruct((rows, mix), F32) for _ in range(7)),
        grid=(rows // tr,),
        in_specs=[row_in, row_in] + [pl.BlockSpec(x.shape, lambda i: (0, 0)) for x in consts],
        out_specs=tuple(row_out for _ in range(7)),
        compiler_params=_cparams("parallel"),
        name="rwkv_prep",
    )(p, prev, *consts)


def _rwkv_scan_kernel(pr_ref, pk_ref, pv_ref, pl_ref, mur_ref, muk_ref, muv_ref, mul_ref,
                      w0_ref, a0_ref, kkw_ref, ka_ref, w2_ref, a2_ref, g2_ref,
                      rk_ref, lg_ref, lbias_ref, mix_ref, st_ref,
                      s_ref, cr_ref, ck_ref, cv_ref, cl_ref,
                      r_ref, k_ref, v_ref, kk_ref, kka_ref, lw_ref, g_ref, *, tb, c, pp):
    n2 = 2 * c
    assert n2 == LANES
    t = pl.program_id(2)

    @pl.when(t == 0)
    def _():
        for ref in (s_ref, cr_ref, ck_ref, cv_ref, cl_ref):
            ref[...] = jnp.zeros_like(ref)

    def shifted(p_ref, carry_ref, mu_ref):
        p = p_ref[...]
        prev = jnp.where(_iota2(p.shape, 0) == 0, carry_ref[...], pltpu.roll(p, 1, axis=0))
        carry_ref[...] = p[tb - 1:tb, :]
        return p + (prev - p) * mu_ref[...]

    xl = shifted(pl_ref, cl_ref, mul_ref)
    la = xl[:, 0:LANES]
    gl = xl[:, LANES:3 * LANES]
    w_raw = -_softplus(-(w0_ref[...] + _bdot(jnp.tanh(la), w2_ref[...]))) - 0.5
    lw_ref[...] = -jnp.exp(w_raw)
    a = _sigmoid(a0_ref[...] + _bdot(la, a2_ref[...]))
    g_ref[...] = _bdot(_sigmoid(gl), g2_ref[...])
    r_ref[...] = shifted(pr_ref, cr_ref, mur_ref)
    v_ref[...] = shifted(pv_ref, cv_ref, muv_ref)
    k_raw = shifted(pk_ref, ck_ref, muk_ref)
    k_ref[...] = k_raw * (1.0 + (a - 1.0) * ka_ref[...])
    for p in range(pp):
        sl = slice(p * LANES, (p + 1) * LANES)
        kk_p = k_raw[:, sl] * kkw_ref[:, sl]
        kk_p = kk_p / jnp.maximum(jnp.sqrt(_seg_sum(kk_p * kk_p, B_HEAD_DIM)), 1e-12)
        kk_ref[:, sl] = kk_p
        kka_ref[:, sl] = kk_p * a[:, sl]

    ri = _iota2((n2, n2), 0)
    ci = _iota2((n2, n2), 1)
    blk = lambda x, n: jnp.right_shift(x, n.bit_length() - 1)
    same_head = blk(ri, c) == blk(ci, c)
    strict = same_head & (ri > ci)
    incl = same_head & (ri >= ci)
    eye = (ri == ci).astype(F32)
    tril = (_iota2((c, c), 0) >= _iota2((c, c), 1)).astype(BF16)
    head0 = _iota2((c, LANES), 1) < B_HEAD_DIM
    hd = float(B_HEAD_DIM)

    def sibling(n):
        return (blk(ri, 2 * n) == blk(ci, 2 * n)) & (blk(ri, n) > blk(ci, n))

    def stack(x):
        return jnp.concatenate([jnp.where(head0, x, 0.0), jnp.where(head0, 0.0, x)], axis=0)

    nch = tb // c
    inst = [(p, ch) for p in range(pp) for ch in range(nch)]
    each = lambda f, *lists: [f(*xs) for xs in zip(*lists)]
    win = lambda ref: [ref[ch * c:(ch + 1) * c, p * LANES:(p + 1) * LANES] for p, ch in inst]
    r, k, v, kk, kka, lw = (win(x) for x in (r_ref, k_ref, v_ref, kk_ref, kka_ref, lw_ref))
    cs = each(lambda x: _exact_ldot(tril, x), lw)
    gend = each(lambda x: jnp.exp(x[c - 1:c, :]), cs)
    ginv = each(lambda x: jnp.exp(-x), cs)
    ph = each(jnp.multiply, kka, ginv)
    kh = each(jnp.multiply, k, ginv)
    kk_m = each(lambda a, b, d: stack(a * jnp.exp(b - d)), kk, cs, lw)
    r_m = each(lambda a, b: stack(a * jnp.exp(b)), r, cs)
    v_m = each(stack, v)
    lhs = each(lambda a, b: jnp.concatenate([a, b], axis=0), kk_m, r_m)
    gp = each(lambda a, b: _bdot_nt(a, stack(b)), lhs, ph)
    gk = each(lambda a, b: _bdot_nt(a, stack(b)), lhs, kh)
    a_mat = each(lambda g: jnp.where(strict, g[0:n2], 0.0), gp)
    b_mat = each(lambda g: jnp.where(strict, g[0:n2], 0.0), gk)
    rp = each(lambda g: jnp.where(incl, g[n2:], 0.0), gp)
    rk = each(lambda g: jnp.where(incl, g[n2:], 0.0), gk)
    bv = each(_bdot, b_mat, v_m)
    rkv = each(_bdot, rk, v_m)
    t_m = each(lambda a: eye - jnp.where(blk(ri, 2) == blk(ci, 2), a, 0.0), a_mat)
    n = 2
    while n < c:
        y = each(lambda a, tm: _bdot(jnp.where(sibling(n), a, 0.0), tm), a_mat, t_m)
        t_m = each(lambda tm, yy: tm - _bdot(tm, yy), t_m, y)
        n *= 2
    ku = each(lambda tm, a, b: _bdot(tm, jnp.concatenate([a, b], axis=1)), t_m, kk_m, bv)
    rpku = each(_bdot, rp, ku)
    ml = each(lambda a, b, g: _bdot_tn(a, stack(b * g)), ku, ph, gend)
    vk = each(lambda a, b, g: _bdot_tn(a, stack(b * g)), v_m, kh, gend)
    r_t = each(lambda a, b: a - b[:, 0:n2], r_m, rpku)
    o_0 = each(lambda a, b: a - b[:, n2:], rkv, rpku)
    cst = each(lambda a, b: a - b[n2:], vk, ml)
    s = [s_ref[p] for p in range(pp)]
    o2 = [None] * len(inst)
    for ch in range(nch):
        for p in range(pp):
            i = p * nch + ch
            o2[i] = _bdot_nt(r_t[i], s[p]) + o_0[i]
            s[p] = s[p] * gend[i] - _bdot(s[p], ml[i][0:n2]) + cst[i]
    for p in range(pp):
        s_ref[p] = s[p]
    for i, (p, ch) in enumerate(inst):
        rows = slice(ch * c, (ch + 1) * c)
        cols = slice(p * LANES, (p + 1) * LANES)
        o = o2[i][0:c, :] + o2[i][c:n2, :]
        d = o - _seg_sum(o, B_HEAD_DIM) / hd
        on = (d * lax.rsqrt(_seg_sum(d * d, B_HEAD_DIM) / hd + GN_EPS) * lg_ref[:, cols]
              + lbias_ref[:, cols])
        bonus = _seg_sum(r[i] * k[i] * rk_ref[:, cols], B_HEAD_DIM) * v[i]
        mix_ref[rows, cols] = ((on + bonus) * g_ref[rows, cols]).astype(mix_ref.dtype)

    @pl.when(t == pl.num_programs(2) - 1)
    def _():
        for p in range(pp):
            s = s_ref[p]
            st_ref[2 * p] = s[0:c, 0:c]
            st_ref[2 * p + 1] = s[c:n2, c:n2]


def _rwkv_scan(p_main, p_lora, params, rk, lg, lbias, nb, t):
    mu, w0, a0, kkw, ka, w2p, a2p, g2p = params
    mix = rk.shape[-1]
    hd = B_HEAD_DIM
    nh = mix // hd
    pp = RW_PAIRS
    wc = pp * LANES
    nblk = mix // wc
    nl = p_lora.shape[1]
    assert nh % (2 * pp) == 0 and p_main.shape[1] == 3 * mix and mu.shape[1] >= 3 * mix + nl
    tb = _tile(t, RW_TB, RW_C)
    nt = t // tb
    col = lambda off: pl.BlockSpec((tb, wc), lambda b, h, i: (b * nt + i, off + h))
    vec = lambda off=0: pl.BlockSpec((1, wc), lambda b, h, i: (0, off + h))
    mat = lambda x: pl.BlockSpec((x.shape[0], wc), lambda b, h, i: (0, h))
    mu_l = lax.slice(mu, (0, 3 * mix), (1, 3 * mix + nl))
    vec1 = lambda x: x.reshape(1, mix)
    return pl.pallas_call(
        functools.partial(_rwkv_scan_kernel, tb=tb, c=RW_C, pp=pp),
        out_shape=(jax.ShapeDtypeStruct((nb * t, mix), BF16),
                   jax.ShapeDtypeStruct((nb, nh, hd, hd), F32)),
        grid=(nb, nblk, nt),
        in_specs=[col(0), col(nblk), col(2 * nblk),
                  pl.BlockSpec((tb, nl), lambda b, h, i: (b * nt + i, 0)),
                  vec(0), vec(nblk), vec(2 * nblk), pl.BlockSpec((1, nl), lambda b, h, i: (0, 0)),
                  vec(), vec(), vec(), vec(), mat(w2p), mat(a2p), mat(g2p), vec(), vec(), vec()],
        out_specs=(col(0), pl.BlockSpec((None, 2 * pp, hd, hd), lambda b, h, i: (b, h, 0, 0))),
        scratch_shapes=([pltpu.VMEM((pp, LANES, LANES), F32)] + [pltpu.VMEM((1, wc), F32)] * 3
                        + [pltpu.VMEM((1, nl), F32)] + [pltpu.VMEM((tb, wc), F32)] * 7),
        compiler_params=_cparams("parallel", "parallel", "arbitrary"),
        name="rwkv_scan",
    )(p_main, p_main, p_main, p_lora, mu, mu, mu, mu_l, w0, a0, kkw, ka, w2p, a2p, g2p,
      vec1(rk), vec1(lg), vec1(lbias))


def _rwkv_sample_kernel(r_ref, k_ref, v_ref, kk_ref, kka_ref, w_ref, g_ref, rk_ref, lg_ref, lbias_ref,
                        s_ref, *rest, layer, first):
    o_ref, so_ref, vt_ref, ot_ref = rest[-4:]
    so_ref = _layer_view(so_ref, layer, first)
    hd = s_ref.shape[1]
    r = r_ref[...]
    k = k_ref[...]
    r_t, k_t, w_t, kka_t = r.T, k.T, w_ref[...].T, kka_ref[...].T
    nkk_t = -(kk_ref[...].T)
    vt_ref[...] = v_ref[...].T
    bon_t = (r * k * rk_ref[...]).T
    for h in range(2):
        sl = slice(h * hd, (h + 1) * hd)
        nkk, kka, w, kf, rr = nkk_t[sl], kka_t[sl], w_t[sl], k_t[sl], r_t[sl]

        def body(i, carry, h=h, nkk=nkk, kka=kka, w=w, kf=kf, rr=rr):
            row = pl.ds(h * hd + i, 1)
            sv = s_ref[h, i]
            sa = jnp.sum(sv * nkk, axis=0, keepdims=True)
            sn = sv * w + sa * kka + vt_ref[row, :] * kf
            so_ref[h, i] = sn
            ot_ref[row, :] = jnp.sum(sn * rr, axis=0, keepdims=True)
            return carry

        lax.fori_loop(0, hd, body, 0, unroll=4)
    dn, bonus = [], []
    for h in range(2):
        sl = slice(h * hd, (h + 1) * hd)
        o = ot_ref[sl, :]
        d = o - jnp.mean(o, axis=0, keepdims=True)
        dn.append(d * lax.rsqrt(jnp.mean(d * d, axis=0, keepdims=True) + GN_EPS))
        bonus.append(jnp.sum(bon_t[sl], axis=0, keepdims=True) * vt_ref[sl, :])
    dn = jnp.concatenate(dn, axis=0).T
    bonus = jnp.concatenate(bonus, axis=0).T
    o_ref[...] = (dn * lg_ref[...] + lbias_ref[...] + bonus) * g_ref[...]


def _rwkv_sample(prep, rk, lg, lbias, states_t, layer, stacked):
    mix = rk.shape[-1]
    hd = B_HEAD_DIM
    nb = prep[0].shape[0]
    assert nb == LANES
    col = pl.BlockSpec((nb, LANES), lambda h: (0, h))
    vec = pl.BlockSpec((1, LANES), lambda h: (0, h))
    st_block = (None, 2, hd, hd, nb)
    st_map = lambda h: (layer, h, 0, 0, 0)
    so_shape, so_spec, alias_specs, alias_args = _stacked_state_io(states_t.shape, st_block, st_map, stacked)
    return pl.pallas_call(
        functools.partial(_rwkv_sample_kernel, layer=layer, first=stacked is None),
        out_shape=(jax.ShapeDtypeStruct((nb, mix), F32), so_shape),
        grid=(mix // LANES,),
        in_specs=[col] * 7 + [vec] * 3 + [pl.BlockSpec(st_block, st_map)] + alias_specs,
        out_specs=(col, so_spec),
        scratch_shapes=[pltpu.VMEM((LANES, nb), F32)] * 2,
        input_output_aliases={11: 1} if alias_args else {},
        compiler_params=_cparams("parallel"),
        name="rwkv_sample",
    )(*prep, rk.reshape(1, mix), lg.reshape(1, mix), lbias.reshape(1, mix), states_t, *alias_args)


def kernel(x_prompt, x_sample, cache_mem_k, cache_mem_v, state_hgrn, state_rwkv, state_rwkv_shift,
           mem_prompt, attn_norm_g, mlp_norm_g, final_norm_g, mem_norm_g, wk_mem, wv_mem,
           a_w_in, a_w_out, a_lb_logits, a_onorm_g,
           b_w_in, b_w_out, b_mu, b_w0, b_w2, b_a0, b_a2, b_g2, b_k_k, b_k_a, b_r_k, b_lnx_g, b_lnx_b,
           mlp_w1, mlp_w2):
    nb, t, d = x_prompt.shape
    ns = x_sample.shape[0]
    depth = attn_norm_g.shape[0]
    mix = a_onorm_g.shape[-1]
    xdim = d - mix
    n_mem = mem_prompt.shape[1]
    b_cols = b_mu.shape[-1]
    b_pad = -(-b_cols // LANES) * LANES
    lora0 = 3 * mix

    lbs = _lower_bounds(a_lb_logits)

    m = _rmsnorm(mem_prompt.reshape(nb * n_mem, d), mem_norm_g, BF16)
    mem_k, mem_v = (a.reshape(depth, nb, n_mem, xdim) for a in _mem_kv(m, wk_mem, wv_mem))
    state_rwkv_t = jnp.transpose(state_rwkv, (0, 2, 3, 4, 1))
    b_w_in_t = jnp.transpose(b_w_in, (0, 2, 1))

    rp = nb * t
    x = jnp.concatenate([x_prompt.reshape(rp, d), x_sample.reshape(ns, d)], axis=0)
    rows_of = lambda a: a[rp:rp + ns]
    hgrn_p, rwkv_p, shift_p, shift_s = [], [], [], []
    hgrn_s = rwkv_s_t = None
    for layer in range(depth):
        j = layer // 2
        g_attn = attn_norm_g[layer]
        h = _rmsnorm(x, g_attn, BF16)
        if layer % 2 == 0:
            p = _matmul(h, a_w_in, layer=j, name="a_in")
            mix_p, st_p = _hgrn_prompt(p, lbs[j], a_onorm_g[j], nb, t)
            mix_s, hgrn_s = _hgrn_sample(rows_of(p), lbs[j], a_onorm_g[j], state_hgrn, j, hgrn_s)
            hgrn_p.append(st_p)
            q_arr, q_blk = p, (4 * mix) // xdim
            qs = rows_of(p)[:, 4 * mix:]
            w_out = a_w_out
        else:
            zeros = lambda n: jnp.zeros((n, mix), F32)
            params = (
                jnp.pad(b_mu[j], (0, b_pad - b_cols)).reshape(1, b_pad),
                b_w0[j].reshape(1, mix), b_a0[j].reshape(1, mix),
                b_k_k[j].reshape(1, mix), b_k_a[j].reshape(1, mix),
                jnp.concatenate([b_w2[j], zeros(LANES - B_DECAY_LORA)], axis=0),
                jnp.concatenate([zeros(B_DECAY_LORA), b_a2[j]], axis=0),
                jnp.concatenate([b_g2[j], zeros(2 * LANES - B_GATE_LORA)], axis=0),
            )
            assert lora0 + B_DECAY_LORA + B_AAA_LORA + B_GATE_LORA == b_cols
            w_q = lax.slice(b_w_in_t, (j, b_cols, 0), (j + 1, b_w_in_t.shape[1], d)).reshape(xdim, d)
            p = _matmul(h, b_w_in_t, layer=j, n_out=lora0, wt=True, name="b_in")
            p_lora = _matmul(h, b_w_in_t, layer=j, n_out=b_pad - lora0, n_off=lora0, wt=True,
                             name="b_lora")
            q_arr, q_blk = _matmul(h, w_q, wt=True, name="b_q"), 0
            qs = rows_of(q_arr)
            ps = jnp.concatenate([rows_of(p), rows_of(p_lora)], axis=1)
            ps_prev = _matmul(state_rwkv_shift[j], b_w_in_t, layer=j, n_out=b_pad, wt=True,
                              name="b_in_shift")
            rk, lg, lbias = b_r_k[j].reshape(mix), b_lnx_g[j], b_lnx_b[j]
            mix_p, st_p = _rwkv_scan(p, p_lora, params, rk, lg, lbias, nb, t)
            mix_s, rwkv_s_t = _rwkv_sample(_rwkv_prep(ps, ps_prev, params), rk, lg, lbias,
                                           state_rwkv_t, j, rwkv_s_t)
            rwkv_p.append(st_p)
            shift_p.append(_rmsnorm(x[t - 1:rp:t], g_attn, F32))
            shift_s.append(_rmsnorm(x, g_attn, F32, rp, ns))
            w_out = b_w_out
        xo_p = _xattn_prompt(q_arr, q_blk, mem_k, mem_v, layer, nb, t)
        xo_s = _xattn_sample(qs, cache_mem_k, cache_mem_v, layer)
        x = _matmul(mix_p, w_out, layer=j, a2=xo_p, res=x, res_row0=0, name="out_prompt")
        x = _matmul(mix_s, w_out, layer=j, a2=xo_s, res=x, res_row0=rp, name="out_sample")
        u = _matmul(_rmsnorm(x, mlp_norm_g[layer], BF16), mlp_w1, layer=layer, act="relu2",
                    out_dtype=BF16, tn_cap=1024, name="mlp1")
        x = _matmul(u, mlp_w2, layer=layer, res=x, tm_cap=MLP2_TM, tn_cap=256, tk_cap=u.shape[1],
                    name="mlp2")

    y_p = _rmsnorm(x, final_norm_g, F32, 0, rp).reshape(nb, t, d)
    y_s = _rmsnorm(x, final_norm_g, F32, rp, ns).reshape(ns, 1, d)
    kv_shape = (depth, nb, n_mem, X_HEADS, X_HEAD_DIM)
    return (y_p, y_s, mem_k.reshape(kv_shape), mem_v.reshape(kv_shape),
            jnp.stack(hgrn_p), jnp.stack(rwkv_p), jnp.stack(shift_p),
            hgrn_s, jnp.transpose(rwkv_s_t, (0, 4, 1, 2, 3)), jnp.stack(shift_s))
```
